```python
import jax, jax.numpy as jnp
from jax import lax
import numpy as np

D_MODEL = 1024
BATCH = 4
SEQ = 8192
DEPTH = 2

MIX_WIDTH = D_MODEL
CONV_CH = MIX_WIDTH // 2
CONV_GROUPS = 8
CONV_KERNEL = 31
HEAD_DIM = 64
N_Q_HEADS = (MIX_WIDTH - CONV_CH) // HEAD_DIM
N_KV_HEADS = 2
KV_GROUP = N_Q_HEADS // N_KV_HEADS
ATTN_WIDTH = N_Q_HEADS * HEAD_DIM
KV_WIDTH = N_KV_HEADS * HEAD_DIM
WINDOW = 128
BLOCK = 128
SPLITS = (CONV_CH, 2 * CONV_CH, 2 * CONV_CH + ATTN_WIDTH,
          2 * CONV_CH + ATTN_WIDTH + KV_WIDTH)
IN_COLS = 2 * CONV_CH + ATTN_WIDTH + 2 * KV_WIDTH
D_FF = 2816
N_EXPERTS = 8
TOP_K = 2
D_FF_EXPERT = 3584
N_DENSE = (DEPTH + 1) // 2
N_MOE = DEPTH // 2
EPS = 1e-5

kernel_name = "hymba_conformer_swa_sink_moe_trunk"


def _rmsnorm(x, g):
    xf = x.astype(jnp.float32)
    y = xf * lax.rsqrt(jnp.mean(xf * xf, axis=-1, keepdims=True) + EPS)
    return (y * g.astype(jnp.float32)).astype(x.dtype)


def _layernorm(x, g, b):
    xf = x.astype(jnp.float32)
    mu = jnp.mean(xf, axis=-1, keepdims=True)
    xc = xf - mu
    y = xc * lax.rsqrt(jnp.mean(xc * xc, axis=-1, keepdims=True) + EPS)
    return (y * g.astype(jnp.float32) + b.astype(jnp.float32)).astype(x.dtype)


def _conformer_conv(a, gate, w_dw, b_dw, ln_g, ln_b):
    u = a * jax.nn.sigmoid(gate)
    u = jnp.pad(u, ((0, 0), (CONV_KERNEL - 1, 0), (0, 0)))
    y = lax.conv_general_dilated(
        u, w_dw[:, None, :], window_strides=(1,), padding='VALID',
        dimension_numbers=('NWC', 'WIO', 'NWC'),
        feature_group_count=CONV_CH) + b_dw
    y = _layernorm(y, ln_g, ln_b)
    return jax.nn.silu(y)


def _swa_sink_attention(q, k, v, sinks):
    B, S = q.shape[0], q.shape[1]
    nb = S // BLOCK
    qb = q.reshape(B, nb, BLOCK, N_KV_HEADS, KV_GROUP, HEAD_DIM)
    kb = k.reshape(B, nb, BLOCK, N_KV_HEADS, HEAD_DIM)
    vb = v.reshape(B, nb, BLOCK, N_KV_HEADS, HEAD_DIM)
    pad = ((0, 0), (1, 0), (0, 0), (0, 0), (0, 0))
    k_band = jnp.concatenate([jnp.pad(kb, pad)[:, :-1], kb], axis=2)
    v_band = jnp.concatenate([jnp.pad(vb, pad)[:, :-1], vb], axis=2)
    scale = HEAD_DIM ** -0.5
    s = jnp.einsum('bnqhgd,bnkhd->bnhgqk', qb, k_band,
                   preferred_element_type=jnp.float32) * scale
    q_pos = jnp.arange(BLOCK)[:, None] + BLOCK
    k_pos = jnp.arange(2 * BLOCK)[None, :]
    rel = q_pos - k_pos
    local = (rel >= 0) & (rel < WINDOW)
    has_prev = (jnp.arange(nb)[:, None, None] > 0) | (k_pos >= BLOCK)[None]
    valid = local[None] & has_prev
    s = jnp.where(valid[None, :, None, None], s, -jnp.inf)
    sink = sinks.astype(jnp.float32).reshape(N_KV_HEADS, KV_GROUP)[None, None, :, :, None, None]
    m = jnp.maximum(jnp.max(s, axis=-1, keepdims=True), sink)
    p = jnp.exp(s - m)
    p = p / (jnp.sum(p, axis=-1, keepdims=True) + jnp.exp(sink - m))
    o = jnp.einsum('bnhgqk,bnkhd->bnqhgd', p.astype(v.dtype), v_band)
    return o.reshape(B, S, ATTN_WIDTH)


def _mixer(h, w_in, b_in, conv_w, conv_b, conv_ln_g, conv_ln_b, sinks, w_out, b_out):
    B, S, _ = h.shape
    z = jnp.einsum('bsd,de->bse', h, w_in) + b_in
    a, gate, q, k, v = jnp.split(z, SPLITS, axis=-1)
    conv_out = _conformer_conv(a, gate, conv_w, conv_b, conv_ln_g, conv_ln_b)
    attn_out = _swa_sink_attention(
        q.reshape(B, S, N_Q_HEADS, HEAD_DIM),
        k.reshape(B, S, N_KV_HEADS, HEAD_DIM),
        v.reshape(B, S, N_KV_HEADS, HEAD_DIM), sinks)
    y = jnp.concatenate([conv_out, attn_out], axis=-1)
    return jnp.einsum('bse,ed->bsd', y, w_out) + b_out


def _swiglu(h, w_gate, w_up, w_down):
    return (jax.nn.silu(h @ w_gate) * (h @ w_up)) @ w_down


def _moe(h, w_router, w_gate, w_up, w_down):
    B, S, D = h.shape
    t = h.reshape(B * S, D)
    logits = (t @ w_router).astype(jnp.float32)
    top_vals, top_idx = lax.top_k(logits, TOP_K)
    top_w = jax.nn.softmax(top_vals, axis=-1)
    gates = jnp.sum(jax.nn.one_hot(top_idx, N_EXPERTS, dtype=jnp.float32)
                    * top_w[..., None], axis=1).astype(t.dtype)
    out = jnp.zeros_like(t)
    for e in range(N_EXPERTS):
        out = out + gates[:, e:e + 1] * _swiglu(t, w_gate[e], w_up[e], w_down[e])
    return out.reshape(B, S, D)


def setup_inputs(seed: int = 0) -> dict:
    key = jax.random.key(seed)
    ks = jax.random.split(key, 24)
    nrm = lambda k, shape, s: jax.random.normal(k, shape, jnp.float32) * s
    d = D_MODEL
    return {
        "x": nrm(ks[0], (BATCH, SEQ, d), 1.0),
        "attn_norm": 1.0 + nrm(ks[1], (DEPTH, d), 0.02),
        "ffn_norm": 1.0 + nrm(ks[2], (DEPTH, d), 0.02),
        "w_in": nrm(ks[3], (DEPTH, d, IN_COLS), d ** -0.5),
        "b_in": nrm(ks[4], (DEPTH, IN_COLS), 0.02),
        "conv_w": nrm(ks[5], (DEPTH, CONV_KERNEL, CONV_CH), CONV_KERNEL ** -0.5),
        "conv_b": nrm(ks[6], (DEPTH, CONV_CH), 0.02),
        "conv_ln_g": 1.0 + nrm(ks[7], (DEPTH, CONV_CH), 0.02),
        "conv_ln_b": nrm(ks[8], (DEPTH, CONV_CH), 0.02),
        "sinks": nrm(ks[9], (DEPTH, N_Q_HEADS), 0.5),
        "w_out": nrm(ks[10], (DEPTH, MIX_WIDTH, d), MIX_WIDTH ** -0.5),
        "b_out": nrm(ks[11], (DEPTH, d), 0.02),
        "ffn_w_gate": nrm(ks[12], (N_DENSE, d, D_FF), d ** -0.5),
        "ffn_w_up": nrm(ks[13], (N_DENSE, d, D_FF), d ** -0.5),
        "ffn_w_down": nrm(ks[14], (N_DENSE, D_FF, d), D_FF ** -0.5),
        "moe_router": nrm(ks[15], (N_MOE, d, N_EXPERTS), d ** -0.5),
        "moe_w_gate": nrm(ks[16], (N_MOE, N_EXPERTS, d, D_FF_EXPERT), d ** -0.5),
        "moe_w_up": nrm(ks[17], (N_MOE, N_EXPERTS, d, D_FF_EXPERT), d ** -0.5),
        "moe_w_down": nrm(ks[18], (N_MOE, N_EXPERTS, D_FF_EXPERT, d), D_FF_EXPERT ** -0.5),
        "final_norm": 1.0 + nrm(ks[19], (d,), 0.02),
    }


def reference(x, attn_norm, ffn_norm, w_in, b_in, conv_w, conv_b, conv_ln_g, conv_ln_b,
              sinks, w_out, b_out, ffn_w_gate, ffn_w_up, ffn_w_down, moe_router,
              moe_w_gate, moe_w_up, moe_w_down, final_norm):
    for l in range(DEPTH):
        h = _rmsnorm(x, attn_norm[l])
        x = x + _mixer(h, w_in[l], b_in[l], conv_w[l], conv_b[l], conv_ln_g[l],
                       conv_ln_b[l], sinks[l], w_out[l], b_out[l])
        h = _rmsnorm(x, ffn_norm[l])
        if l % 2 == 0:
            i = l // 2
            x = x + _swiglu(h, ffn_w_gate[i], ffn_w_up[i], ffn_w_down[i])
        else:
            i = l // 2
            x = x + _moe(h, moe_router[i], moe_w_gate[i], moe_w_up[i], moe_w_down[i])
    return _rmsnorm(x, final_norm)
```

```python
import functools

import jax
import jax.numpy as jnp
from jax import lax
from jax.experimental import pallas as pl
from jax.experimental.pallas import tpu as pltpu

F32 = jnp.float32
BF16 = jnp.bfloat16

D_MODEL = 1024
CONV_CH = 512
CONV_KERNEL = 31
HEAD_DIM = 64
N_Q_HEADS = 8
N_KV_HEADS = 2
ATTN_WIDTH = N_Q_HEADS * HEAD_DIM
KV_WIDTH = N_KV_HEADS * HEAD_DIM
BLOCK = 128
N_EXPERTS = 8
EPS = 1e-5

LANES = 128
CONV_HALO = 32
CONV_ROWS = 64
VMEM_LIMIT = 56 * 1024 * 1024

COL_Q = 2 * CONV_CH
COL_K = COL_Q + ATTN_WIDTH
COL_V = COL_K + KV_WIDTH
IN_COLS = COL_V + KV_WIDTH


def _rms(x, g):
    ms = jnp.mean(x * x, axis=-1, keepdims=True)
    return x * lax.rsqrt(ms + EPS) * g


def _silu(x):
    return x * jax.nn.sigmoid(x)


def _mixer_kernel(x_ref, g_ref, win_ref, bin_ref, cw_ref, cb_ref, lng_ref, lnb_ref,
                  sink_ref, wout_ref, bout_ref, o_ref, ubuf, kbuf, vbuf, ybuf, *, ts):
    first = pl.program_id(1) == 0
    x = x_ref[0]
    h = _rms(x, g_ref[...]).astype(BF16)

    ag = jnp.dot(h, win_ref[:, 0:COL_Q], preferred_element_type=F32) + bin_ref[:, 0:COL_Q]
    u = ag[:, :CONV_CH] * jax.nn.sigmoid(ag[:, CONV_CH:])

    @pl.when(first)
    def _():
        ubuf[0:CONV_HALO, :] = jnp.zeros((CONV_HALO, CONV_CH), F32)

    @pl.when(jnp.logical_not(first))
    def _():
        ubuf[0:CONV_HALO, :] = ubuf[ts:ts + CONV_HALO, :]

    ubuf[CONV_HALO:CONV_HALO + ts, :] = u

    tap0 = CONV_HALO - (CONV_KERNEL - 1)

    def conv_chunk(i, carry):
        r0 = pl.multiple_of(i * CONV_ROWS, CONV_ROWS)
        win = ubuf[pl.ds(r0, CONV_ROWS + CONV_HALO), :]
        acc = jnp.broadcast_to(cb_ref[...], (CONV_ROWS, CONV_CH))
        for j in range(CONV_KERNEL):
            acc = acc + cw_ref[j:j + 1, :] * win[tap0 + j:tap0 + j + CONV_ROWS, :]
        mu = jnp.mean(acc, axis=-1, keepdims=True)
        xc = acc - mu
        var = jnp.mean(xc * xc, axis=-1, keepdims=True)
        yn = xc * lax.rsqrt(var + EPS) * lng_ref[...] + lnb_ref[...]
        ybuf[pl.ds(r0, CONV_ROWS), 0:CONV_CH] = _silu(yn).astype(BF16)
        return carry

    lax.fori_loop(0, ts // CONV_ROWS, conv_chunk, 0)

    q = jnp.dot(h, win_ref[:, COL_Q:COL_K], preferred_element_type=F32) + bin_ref[:, COL_Q:COL_K]
    q = (q * (HEAD_DIM ** -0.5)).astype(BF16)
    kv = jnp.dot(h, win_ref[:, COL_K:IN_COLS], preferred_element_type=F32) + bin_ref[:, COL_K:IN_COLS]

    @pl.when(first)
    def _():
        kbuf[0:BLOCK, :] = jnp.zeros((BLOCK, 4 * LANES), BF16)
        vbuf[0:BLOCK, :] = jnp.zeros((BLOCK, 4 * LANES), BF16)

    @pl.when(jnp.logical_not(first))
    def _():
        kbuf[0:BLOCK, :] = kbuf[ts:ts + BLOCK, :]
        vbuf[0:BLOCK, :] = vbuf[ts:ts + BLOCK, :]

    lane = lax.broadcasted_iota(jnp.int32, (ts, LANES), 1)
    low = lane < HEAD_DIM
    for src, buf in ((kv[:, 0:LANES], kbuf), (kv[:, LANES:2 * LANES], vbuf)):
        swapped = pltpu.roll(src, HEAD_DIM, axis=1)
        zero = jnp.zeros_like(src)
        buf[BLOCK:BLOCK + ts, 0 * LANES:1 * LANES] = jnp.where(low, src, zero).astype(BF16)
        buf[BLOCK:BLOCK + ts, 1 * LANES:2 * LANES] = jnp.where(low, zero, swapped).astype(BF16)
        buf[BLOCK:BLOCK + ts, 2 * LANES:3 * LANES] = jnp.where(low, swapped, zero).astype(BF16)
        buf[BLOCK:BLOCK + ts, 3 * LANES:4 * LANES] = jnp.where(low, zero, src).astype(BF16)

    qi = lax.broadcasted_iota(jnp.int32, (BLOCK, 2 * BLOCK), 0)
    kj = lax.broadcasted_iota(jnp.int32, (BLOCK, 2 * BLOCK), 1)
    band = (kj > qi) & (kj <= qi + BLOCK)
    band_first = band & ((kj >= BLOCK) | jnp.logical_not(first))
    low_o = lax.broadcasted_iota(jnp.int32, (BLOCK, LANES), 1) < HEAD_DIM

    for n in range(ts // BLOCK):
        rows = slice(n * BLOCK, (n + 2) * BLOCK)
        mask = band_first if n == 0 else band
        for hkv in range(N_KV_HEADS):
            k_bd = jnp.concatenate([kbuf[rows, (2 * hkv) * LANES:(2 * hkv + 1) * LANES],
                                    kbuf[rows, (2 * hkv + 1) * LANES:(2 * hkv + 2) * LANES]], axis=0)
            v_bd = jnp.concatenate([vbuf[rows, (2 * hkv) * LANES:(2 * hkv + 1) * LANES],
                                    vbuf[rows, (2 * hkv + 1) * LANES:(2 * hkv + 2) * LANES]], axis=0)
            for pair in range(2):
                hp = 2 * hkv + pair
                qp = q[n * BLOCK:(n + 1) * BLOCK, hp * LANES:(hp + 1) * LANES]
                s = lax.dot_general(qp, k_bd, (((1,), (1,)), ((), ())),
                                    preferred_element_type=F32)
                ps, rden = [], []
                for hh in range(2):
                    sink = sink_ref[2 * hp + hh]
                    sh = jnp.where(mask, s[:, hh * 2 * BLOCK:(hh + 1) * 2 * BLOCK], -jnp.inf)
                    m = jnp.maximum(jnp.max(sh, axis=-1, keepdims=True), sink)
                    p = jnp.exp(sh - m)
                    den = jnp.sum(p, axis=-1, keepdims=True) + jnp.exp(sink - m)
                    ps.append(p.astype(BF16))
                    rden.append(1.0 / den)
                o = jnp.dot(jnp.concatenate(ps, axis=1), v_bd, preferred_element_type=F32)
                o = o * jnp.where(low_o, rden[0], rden[1])
                ybuf[n * BLOCK:(n + 1) * BLOCK, CONV_CH + hp * LANES:CONV_CH + (hp + 1) * LANES] = o.astype(BF16)

    y = jnp.dot(ybuf[...], wout_ref[...], preferred_element_type=F32)
    o_ref[0] = x + y + bout_ref[...]


def _mixer(x, g, w_in, b_in, conv_w, conv_b, ln_g, ln_b, sinks, w_out, b_out, *, ts):
    B, S, D = x.shape
    row = lambda a: a.reshape(1, -1)
    const = lambda shape: pl.BlockSpec(shape, lambda b, s: (0,) * len(shape))
    return pl.pallas_call(
        functools.partial(_mixer_kernel, ts=ts),
        grid=(B, S // ts),
        in_specs=[
            pl.BlockSpec((1, ts, D), lambda b, s: (b, s, 0)),
            const((1, D)),
            const((D, IN_COLS)),
            const((1, IN_COLS)),
            const((CONV_KERNEL, CONV_CH)),
            const((1, CONV_CH)),
            const((1, CONV_CH)),
            const((1, CONV_CH)),
            pl.BlockSpec(memory_space=pltpu.SMEM),
            const((D, D)),
            const((1, D)),
        ],
        out_specs=pl.BlockSpec((1, ts, D), lambda b, s: (b, s, 0)),
        out_shape=jax.ShapeDtypeStruct((B, S, D), F32),
        scratch_shapes=[
            pltpu.VMEM((CONV_HALO + ts, CONV_CH), F32),
            pltpu.VMEM((BLOCK + ts, 4 * LANES), BF16),
            pltpu.VMEM((BLOCK + ts, 4 * LANES), BF16),
            pltpu.VMEM((ts, D), BF16),
        ],
        compiler_params=pltpu.CompilerParams(
            dimension_semantics=("arbitrary", "arbitrary"), vmem_limit_bytes=VMEM_LIMIT),
        name="mixer",
    )(x, row(g), w_in.astype(BF16), row(b_in), conv_w, row(conv_b), row(ln_g), row(ln_b),
      sinks, w_out.astype(BF16), row(b_out))


def _ffn_kernel(x_ref, g_ref, wg_ref, wu_ref, wd_ref, o_ref, h_scr, acc_scr):
    j = pl.program_id(1)

    @pl.when(j == 0)
    def _():
        x = x_ref[...]
        h_scr[...] = _rms(x, g_ref[...]).astype(BF16)
        acc_scr[...] = x

    h = h_scr[...]
    gate = jnp.dot(h, wg_ref[...], preferred_element_type=F32)
    up = jnp.dot(h, wu_ref[...], preferred_element_type=F32)
    a = (_silu(gate) * up).astype(BF16)
    acc_scr[...] += jnp.dot(a, wd_ref[...], preferred_element_type=F32)

    @pl.when(j == pl.num_programs(1) - 1)
    def _():
        o_ref[...] = acc_scr[...]


def _ffn(x, g, w_gate, w_up, w_down, *, tm, tf):
    T, D = x.shape
    F = w_gate.shape[1]
    return pl.pallas_call(
        _ffn_kernel,
        grid=(T // tm, F // tf),
        in_specs=[
            pl.BlockSpec((tm, D), lambda i, j: (i, 0)),
            pl.BlockSpec((1, D), lambda i, j: (0, 0)),
            pl.BlockSpec((D, tf), lambda i, j: (0, j)),
            pl.BlockSpec((D, tf), lambda i, j: (0, j)),
            pl.BlockSpec((tf, D), lambda i, j: (j, 0)),
        ],
        out_specs=pl.BlockSpec((tm, D), lambda i, j: (i, 0)),
        out_shape=jax.ShapeDtypeStruct((T, D), F32),
        scratch_shapes=[pltpu.VMEM((tm, D), BF16), pltpu.VMEM((tm, D), F32)],
        compiler_params=pltpu.CompilerParams(
            dimension_semantics=("arbitrary", "arbitrary"), vmem_limit_bytes=VMEM_LIMIT),
        name="dense_ffn",
    )(x, g.reshape(1, D), w_gate.astype(BF16), w_up.astype(BF16), w_down.astype(BF16))


def _split_bf16(a):
    hi = a.astype(BF16)
    lo = (a - hi.astype(F32)).astype(BF16)
    return hi, lo


def _moe_kernel(x_ref, g_ref, wr_hi_ref, wr_lo_ref, wg_ref, wu_ref, wd_ref, fg_ref, o_ref,
                h_scr, gates_scr, acc_scr):
    e = pl.program_id(1)
    j = pl.program_id(2)
    tm = x_ref.shape[0]
    lane = lax.broadcasted_iota(jnp.int32, (tm, LANES), 1)

    @pl.when((e == 0) & (j == 0))
    def _():
        x = x_ref[...]
        hf = _rms(x, g_ref[...])
        h_hi, h_lo = _split_bf16(hf)
        h_scr[...] = h_hi
        logits = (jnp.dot(h_hi, wr_hi_ref[...], preferred_element_type=F32)
                  + jnp.dot(h_lo, wr_hi_ref[...], preferred_element_type=F32)
                  + jnp.dot(h_hi, wr_lo_ref[...], preferred_element_type=F32))
        lg = jnp.where(lane < N_EXPERTS, logits, -jnp.inf)
        v1 = jnp.max(lg, axis=-1, keepdims=True)
        i1 = jnp.min(jnp.where(lg == v1, lane, LANES), axis=-1, keepdims=True)
        lg2 = jnp.where(lane == i1, -jnp.inf, lg)
        v2 = jnp.max(lg2, axis=-1, keepdims=True)
        i2 = jnp.min(jnp.where(lg2 == v2, lane, LANES), axis=-1, keepdims=True)
        t = jnp.exp(v2 - v1)
        w1 = 1.0 / (1.0 + t)
        w2 = t / (1.0 + t)
        gates_scr[...] = jnp.where(lane == i1, w1, 0.0) + jnp.where(lane == i2, w2, 0.0)
        acc_scr[...] = x

    gate_e = jnp.sum(jnp.where(lane == e, gates_scr[...], 0.0), axis=-1, keepdims=True)
    h = h_scr[...]
    gate = jnp.dot(h, wg_ref[0], preferred_element_type=F32)
    up = jnp.dot(h, wu_ref[0], preferred_element_type=F32)
    a = (_silu(gate) * up * gate_e).astype(BF16)
    acc_scr[...] += jnp.dot(a, wd_ref[0], preferred_element_type=F32)

    @pl.when((e == pl.num_programs(1) - 1) & (j == pl.num_programs(2) - 1))
    def _():
        o_ref[...] = _rms(acc_scr[...], fg_ref[...])


def _moe(x, g, w_router, w_gate, w_up, w_down, final_g, *, tm, tf):
    T, D = x.shape
    E, _, F = w_gate.shape
    wr = jnp.pad(w_router, ((0, 0), (0, LANES - E)))
    wr_hi = wr.astype(BF16)
    wr_lo = (wr - wr_hi.astype(F32)).astype(BF16)
    return pl.pallas_call(
        _moe_kernel,
        grid=(T // tm, E, F // tf),
        in_specs=[
            pl.BlockSpec((tm, D), lambda i, e, j: (i, 0)),
            pl.BlockSpec((1, D), lambda i, e, j: (0, 0)),
            pl.BlockSpec((D, LANES), lambda i, e, j: (0, 0)),
            pl.BlockSpec((D, LANES), lambda i, e, j: (0, 0)),
            pl.BlockSpec((1, D, tf), lambda i, e, j: (e, 0, j)),
            pl.BlockSpec((1, D, tf), lambda i, e, j: (e, 0, j)),
            pl.BlockSpec((1, tf, D), lambda i, e, j: (e, j, 0)),
            pl.BlockSpec((1, D), lambda i, e, j: (0, 0)),
        ],
        out_specs=pl.BlockSpec((tm, D), lambda i, e, j: (i, 0)),
        out_shape=jax.ShapeDtypeStruct((T, D), F32),
        scratch_shapes=[pltpu.VMEM((tm, D), BF16), pltpu.VMEM((tm, LANES), F32), pltpu.VMEM((tm, D), F32)],
        compiler_params=pltpu.CompilerParams(
            dimension_semantics=("arbitrary", "arbitrary", "arbitrary"), vmem_limit_bytes=VMEM_LIMIT),
        name="moe",
    )(x, g.reshape(1, D), wr_hi, wr_lo, w_gate.astype(BF16), w_up.astype(BF16), w_down.astype(BF16),
      final_g.reshape(1, D))


def kernel(x, attn_norm, ffn_norm, w_in, b_in, conv_w, conv_b, conv_ln_g, conv_ln_b, sinks, w_out, b_out,
           ffn_w_gate, ffn_w_up, ffn_w_down, moe_router, moe_w_gate, moe_w_up, moe_w_down, final_norm):
    B, S, D = x.shape
    ts = min(S, 512)
    tm = min(B * S, 512)

    def mixer(x, l):
        return _mixer(x, attn_norm[l], w_in[l], b_in[l], conv_w[l], conv_b[l], conv_ln_g[l], conv_ln_b[l],
                      sinks[l], w_out[l], b_out[l], ts=ts)

    x = mixer(x, 0)
    x = _ffn(x.reshape(B * S, D), ffn_norm[0], ffn_w_gate[0], ffn_w_up[0], ffn_w_down[0],
             tm=tm, tf=ffn_w_gate.shape[2] // 2)
    x = mixer(x.reshape(B, S, D), 1)
    x = _moe(x.reshape(B * S, D), ffn_norm[1], moe_router[0], moe_w_gate[0], moe_w_up[0], moe_w_down[0],
             final_norm, tm=tm, tf=moe_w_gate.shape[3] // 2)
    return x.reshape(B, S, D)
```

```python
import functools

import jax
import jax.numpy as jnp
from jax import lax
from jax.experimental import pallas as pl
from jax.experimental.pallas import tpu as pltpu

F32 = jnp.float32
BF16 = jnp.bfloat16
I32 = jnp.int32

D_MODEL = 1024
CONV_CH = 512
CONV_KERNEL = 31
HEAD_DIM = 64
N_Q_HEADS = 8
N_KV_HEADS = 2
ATTN_WIDTH = N_Q_HEADS * HEAD_DIM
KV_WIDTH = N_KV_HEADS * HEAD_DIM
BLOCK = 128
N_EXPERTS = 8
EPS = 1e-5

LANES = 128
SUBLANES = 8
CONV_HALO = 32
CONV_ROWS = 64
VMEM_LIMIT = 56 * 1024 * 1024

COL_Q = 2 * CONV_CH
COL_K = COL_Q + ATTN_WIDTH
COL_V = COL_K + KV_WIDTH
IN_COLS = COL_V + KV_WIDTH

ROW_SUB = D_MODEL // LANES
KEY_SHIFT = 20
KEY_MASK = (1 << KEY_SHIFT) - 1
DMA_UNROLL = 8


def _rms(x, g):
    ms = jnp.mean(x * x, axis=-1, keepdims=True)
    return x * lax.rsqrt(ms + EPS) * g


def _silu(x):
    return x * jax.nn.sigmoid(x)


def _mixer_kernel(x_ref, g_ref, win_ref, bin_ref, cw_ref, cb_ref, lng_ref, lnb_ref,
                  sink_ref, wout_ref, bout_ref, o_ref, ubuf, kbuf, vbuf, ybuf, *, ts):
    first = pl.program_id(1) == 0
    x = x_ref[0]
    h = _rms(x, g_ref[...]).astype(BF16)

    ag = jnp.dot(h, win_ref[:, 0:COL_Q], preferred_element_type=F32) + bin_ref[:, 0:COL_Q]
    u = ag[:, :CONV_CH] * jax.nn.sigmoid(ag[:, CONV_CH:])

    @pl.when(first)
    def _():
        ubuf[0:CONV_HALO, :] = jnp.zeros((CONV_HALO, CONV_CH), F32)

    @pl.when(jnp.logical_not(first))
    def _():
        ubuf[0:CONV_HALO, :] = ubuf[ts:ts + CONV_HALO, :]

    ubuf[CONV_HALO:CONV_HALO + ts, :] = u

    tap0 = CONV_HALO - (CONV_KERNEL - 1)

    def conv_chunk(i, carry):
        r0 = pl.multiple_of(i * CONV_ROWS, CONV_ROWS)
        win = ubuf[pl.ds(r0, CONV_ROWS + CONV_HALO), :]
        acc = jnp.broadcast_to(cb_ref[...], (CONV_ROWS, CONV_CH))
        for j in range(CONV_KERNEL):
            acc = acc + cw_ref[j:j + 1, :] * win[tap0 + j:tap0 + j + CONV_ROWS, :]
        mu = jnp.mean(acc, axis=-1, keepdims=True)
        xc = acc - mu
        var = jnp.mean(xc * xc, axis=-1, keepdims=True)
        yn = xc * lax.rsqrt(var + EPS) * lng_ref[...] + lnb_ref[...]
        ybuf[pl.ds(r0, CONV_ROWS), 0:CONV_CH] = _silu(yn).astype(BF16)
        return carry

    lax.fori_loop(0, ts // CONV_ROWS, conv_chunk, 0)

    q = jnp.dot(h, win_ref[:, COL_Q:COL_K], preferred_element_type=F32) + bin_ref[:, COL_Q:COL_K]
    q = (q * (HEAD_DIM ** -0.5)).astype(BF16)
    kv = jnp.dot(h, win_ref[:, COL_K:IN_COLS], preferred_element_type=F32) + bin_ref[:, COL_K:IN_COLS]

    @pl.when(first)
    def _():
        kbuf[0:BLOCK, :] = jnp.zeros((BLOCK, 4 * LANES), BF16)
        vbuf[0:BLOCK, :] = jnp.zeros((BLOCK, 4 * LANES), BF16)

    @pl.when(jnp.logical_not(first))
    def _():
        kbuf[0:BLOCK, :] = kbuf[ts:ts + BLOCK, :]
        vbuf[0:BLOCK, :] = vbuf[ts:ts + BLOCK, :]

    lane = lax.broadcasted_iota(I32, (ts, LANES), 1)
    low = lane < HEAD_DIM
    for src, buf in ((kv[:, 0:LANES], kbuf), (kv[:, LANES:2 * LANES], vbuf)):
        swapped = pltpu.roll(src, HEAD_DIM, axis=1)
        zero = jnp.zeros_like(src)
        buf[BLOCK:BLOCK + ts, 0 * LANES:1 * LANES] = jnp.where(low, src, zero).astype(BF16)
        buf[BLOCK:BLOCK + ts, 1 * LANES:2 * LANES] = jnp.where(low, zero, swapped).astype(BF16)
        buf[BLOCK:BLOCK + ts, 2 * LANES:3 * LANES] = jnp.where(low, swapped, zero).astype(BF16)
        buf[BLOCK:BLOCK + ts, 3 * LANES:4 * LANES] = jnp.where(low, zero, src).astype(BF16)

    qi = lax.broadcasted_iota(I32, (BLOCK, 2 * BLOCK), 0)
    kj = lax.broadcasted_iota(I32, (BLOCK, 2 * BLOCK), 1)
    band = (kj > qi) & (kj <= qi + BLOCK)
    band_first = band & ((kj >= BLOCK) | jnp.logical_not(first))
    low_o = lax.broadcasted_iota(I32, (BLOCK, LANES), 1) < HEAD_DIM

    for n in range(ts // BLOCK):
        rows = slice(n * BLOCK, (n + 2) * BLOCK)
        mask = band_first if n == 0 else band
        for hkv in range(N_KV_HEADS):
            k_bd = jnp.concatenate([kbuf[rows, (2 * hkv) * LANES:(2 * hkv + 1) * LANES],
                                    kbuf[rows, (2 * hkv + 1) * LANES:(2 * hkv + 2) * LANES]], axis=0)
            v_bd = jnp.concatenate([vbuf[rows, (2 * hkv) * LANES:(2 * hkv + 1) * LANES],
                                    vbuf[rows, (2 * hkv + 1) * LANES:(2 * hkv + 2) * LANES]], axis=0)
            for pair in range(2):
                hp = 2 * hkv + pair
                qp = q[n * BLOCK:(n + 1) * BLOCK, hp * LANES:(hp + 1) * LANES]
                s = lax.dot_general(qp, k_bd, (((1,), (1,)), ((), ())),
                                    preferred_element_type=F32)
                ps, rden = [], []
                for hh in range(2):
                    sink = sink_ref[2 * hp + hh]
                    sh = jnp.where(mask, s[:, hh * 2 * BLOCK:(hh + 1) * 2 * BLOCK], -jnp.inf)
                    m = jnp.maximum(jnp.max(sh, axis=-1, keepdims=True), sink)
                    p = jnp.exp(sh - m)
                    den = jnp.sum(p, axis=-1, keepdims=True) + jnp.exp(sink - m)
                    ps.append(p.astype(BF16))
                    rden.append(1.0 / den)
                o = jnp.dot(jnp.concatenate(ps, axis=1), v_bd, preferred_element_type=F32)
                o = o * jnp.where(low_o, rden[0], rden[1])
                ybuf[n * BLOCK:(n + 1) * BLOCK, CONV_CH + hp * LANES:CONV_CH + (hp + 1) * LANES] = o.astype(BF16)

    y = jnp.dot(ybuf[...], wout_ref[...], preferred_element_type=F32)
    o_ref[0] = x + y + bout_ref[...]


def _mixer(x, g, w_in, b_in, conv_w, conv_b, ln_g, ln_b, sinks, w_out, b_out, *, ts):
    B, S, D = x.shape
    row = lambda a: a.reshape(1, -1)
    const = lambda shape: pl.BlockSpec(shape, lambda b, s: (0,) * len(shape))
    return pl.pallas_call(
        functools.partial(_mixer_kernel, ts=ts),
        grid=(B, S // ts),
        in_specs=[
            pl.BlockSpec((1, ts, D), lambda b, s: (b, s, 0)),
            const((1, D)),
            const((D, IN_COLS)),
            const((1, IN_COLS)),
            const((CONV_KERNEL, CONV_CH)),
            const((1, CONV_CH)),
            const((1, CONV_CH)),
            const((1, CONV_CH)),
            pl.BlockSpec(memory_space=pltpu.SMEM),
            const((D, D)),
            const((1, D)),
        ],
        out_specs=pl.BlockSpec((1, ts, D), lambda b, s: (b, s, 0)),
        out_shape=jax.ShapeDtypeStruct((B, S, D), F32),
        scratch_shapes=[
            pltpu.VMEM((CONV_HALO + ts, CONV_CH), F32),
            pltpu.VMEM((BLOCK + ts, 4 * LANES), BF16),
            pltpu.VMEM((BLOCK + ts, 4 * LANES), BF16),
            pltpu.VMEM((ts, D), BF16),
        ],
        compiler_params=pltpu.CompilerParams(
            dimension_semantics=("arbitrary", "arbitrary"), vmem_limit_bytes=VMEM_LIMIT),
        name="mixer",
    )(x, row(g), w_in.astype(BF16), row(b_in), conv_w, row(conv_b), row(ln_g), row(ln_b),
      sinks, w_out.astype(BF16), row(b_out))


def _ffn_kernel(x_ref, g_ref, wg_ref, wu_ref, wd_ref, o_ref, h_scr, acc_scr):
    j = pl.program_id(1)

    @pl.when(j == 0)
    def _():
        x = x_ref[...]
        h_scr[...] = _rms(x, g_ref[...]).astype(BF16)
        acc_scr[...] = x

    h = h_scr[...]
    gate = jnp.dot(h, wg_ref[...], preferred_element_type=F32)
    up = jnp.dot(h, wu_ref[...], preferred_element_type=F32)
    a = (_silu(gate) * up).astype(BF16)
    acc_scr[...] += jnp.dot(a, wd_ref[...], preferred_element_type=F32)

    @pl.when(j == pl.num_programs(1) - 1)
    def _():
        o_ref[...] = acc_scr[...]


def _ffn(x, g, w_gate, w_up, w_down, *, tm, tf):
    T, D = x.shape
    F = w_gate.shape[1]
    return pl.pallas_call(
        _ffn_kernel,
        grid=(T // tm, F // tf),
        in_specs=[
            pl.BlockSpec((tm, D), lambda i, j: (i, 0)),
            pl.BlockSpec((1, D), lambda i, j: (0, 0)),
            pl.BlockSpec((D, tf), lambda i, j: (0, j)),
            pl.BlockSpec((D, tf), lambda i, j: (0, j)),
            pl.BlockSpec((tf, D), lambda i, j: (j, 0)),
        ],
        out_specs=pl.BlockSpec((tm, D), lambda i, j: (i, 0)),
        out_shape=jax.ShapeDtypeStruct((T, D), F32),
        scratch_shapes=[pltpu.VMEM((tm, D), BF16), pltpu.VMEM((tm, D), F32)],
        compiler_params=pltpu.CompilerParams(
            dimension_semantics=("arbitrary", "arbitrary"), vmem_limit_bytes=VMEM_LIMIT),
        name="dense_ffn",
    )(x, g.reshape(1, D), w_gate.astype(BF16), w_up.astype(BF16), w_down.astype(BF16))


def _stage_store(stage_ref, val, accumulate=False):
    n = val.shape[0]
    for c in range(ROW_SUB):
        tile = val[:, c * LANES:(c + 1) * LANES].reshape(n // SUBLANES, SUBLANES, LANES)
        if accumulate:
            stage_ref[:, c * SUBLANES:(c + 1) * SUBLANES, :] += tile
        else:
            stage_ref[:, c * SUBLANES:(c + 1) * SUBLANES, :] = tile


def _stage_load(stage_ref):
    n = stage_ref.shape[0] * SUBLANES
    return jnp.concatenate(
        [stage_ref[:, c * SUBLANES:(c + 1) * SUBLANES, :].reshape(n, LANES) for c in range(ROW_SUB)], axis=1)


def _stage_to_rows(stage_ref, rows_ref):
    def body(g, carry):
        for t in range(SUBLANES):
            rows_ref[g * SUBLANES + t] = stage_ref[g, pl.ds(t, ROW_SUB, stride=SUBLANES), :]
        return carry

    lax.fori_loop(0, stage_ref.shape[0], body, 0)


def _rows_to_stage(rows_ref, stage_ref):
    def body(g, carry):
        for t in range(SUBLANES):
            stage_ref[g, pl.ds(t, ROW_SUB, stride=SUBLANES), :] = rows_ref[g * SUBLANES + t]
        return carry

    lax.fori_loop(0, stage_ref.shape[0], body, 0)


def _split_bf16(a):
    hi = a.astype(BF16)
    lo = (a - hi.astype(F32)).astype(BF16)
    return hi, lo


def _router_kernel(x_ref, g_ref, wr_hi_ref, wr_lo_ref, h3_ref, keys_ref, wts_ref, cnt_ref, stage, count):
    tm = x_ref.shape[0]
    lane = lax.broadcasted_iota(I32, (tm, LANES), 1)

    @pl.when(pl.program_id(0) == 0)
    def _():
        count[...] = jnp.zeros_like(count)

    hf = _rms(x_ref[...], g_ref[...])
    h_hi, h_lo = _split_bf16(hf)
    logits = (jnp.dot(h_hi, wr_hi_ref[...], preferred_element_type=F32)
              + jnp.dot(h_lo, wr_hi_ref[...], preferred_element_type=F32)
              + jnp.dot(h_hi, wr_lo_ref[...], preferred_element_type=F32))
    lg = jnp.where(lane < N_EXPERTS, logits, -jnp.inf)
    v1 = jnp.max(lg, axis=-1, keepdims=True)
    i1 = jnp.min(jnp.where(lg == v1, lane, LANES), axis=-1, keepdims=True)
    lg2 = jnp.where(lane == i1, -jnp.inf, lg)
    v2 = jnp.max(lg2, axis=-1, keepdims=True)
    i2 = jnp.min(jnp.where(lg2 == v2, lane, LANES), axis=-1, keepdims=True)
    t = jnp.exp(v2 - v1)
    w1 = 1.0 / (1.0 + t)
    w2 = t / (1.0 + t)

    onehot = jnp.where((lane == i1) | (lane == i2), 1.0, 0.0)
    r = lax.broadcasted_iota(I32, (tm, tm), 0)
    c = lax.broadcasted_iota(I32, (tm, tm), 1)
    earlier = jnp.where(c < r, 1.0, 0.0).astype(BF16)
    before = count[...] + jnp.dot(earlier, onehot.astype(BF16), preferred_element_type=F32)
    rank1 = jnp.sum(jnp.where(lane == i1, before, 0.0), axis=-1, keepdims=True).astype(I32)
    rank2 = jnp.sum(jnp.where(lane == i2, before, 0.0), axis=-1, keepdims=True).astype(I32)
    count[...] += jnp.sum(onehot, axis=0, keepdims=True)

    key1 = i1 * (1 << KEY_SHIFT) + rank1
    key2 = i2 * (1 << KEY_SHIFT) + rank2
    keys_ref[...] = jnp.where(lane == 0, key1, jnp.where(lane == 1, key2, 0))
    wts_ref[...] = jnp.where(lane == 0, w1, jnp.where(lane == 1, w2, 0.0))
    cnt_ref[...] = count[...].astype(I32)
    _stage_store(stage, hf)
    _stage_to_rows(stage, h3_ref)


def _router(x, g, w_router, *, tm):
    T, D = x.shape
    wr = jnp.pad(w_router, ((0, 0), (0, LANES - N_EXPERTS)))
    wr_hi = wr.astype(BF16)
    wr_lo = (wr - wr_hi.astype(F32)).astype(BF16)
    return pl.pallas_call(
        _router_kernel,
        grid=(T // tm,),
        in_specs=[
            pl.BlockSpec((tm, D), lambda i: (i, 0)),
            pl.BlockSpec((1, D), lambda i: (0, 0)),
            pl.BlockSpec((D, LANES), lambda i: (0, 0)),
            pl.BlockSpec((D, LANES), lambda i: (0, 0)),
        ],
        out_specs=[
            pl.BlockSpec((tm, ROW_SUB, LANES), lambda i: (i, 0, 0)),
            pl.BlockSpec((tm, LANES), lambda i: (i, 0)),
            pl.BlockSpec((tm, LANES), lambda i: (i, 0)),
            pl.BlockSpec((1, LANES), lambda i: (0, 0)),
        ],
        out_shape=[
            jax.ShapeDtypeStruct((T, ROW_SUB, LANES), F32),
            jax.ShapeDtypeStruct((T, LANES), I32),
            jax.ShapeDtypeStruct((T, LANES), F32),
            jax.ShapeDtypeStruct((1, LANES), I32),
        ],
        scratch_shapes=[pltpu.VMEM((tm // SUBLANES, ROW_SUB * SUBLANES, LANES), F32),
                        pltpu.VMEM((1, LANES), F32)],
        compiler_params=pltpu.CompilerParams(dimension_semantics=("arbitrary",), vmem_limit_bytes=VMEM_LIMIT),
        name="router",
    )(x, g.reshape(1, D), wr_hi, wr_lo)


def _row_position(key_ref, starts_ref, t):
    key = key_ref[t]
    return starts_ref[key >> KEY_SHIFT] + (key & KEY_MASK)


def _dispatch_kernel(key1_ref, key2_ref, starts_ref, h3_ref, xs_in_ref, xs_ref, sem):
    del xs_in_ref
    tm = h3_ref.shape[0]
    base = pl.program_id(0) * tm

    def row_copy(t, key_ref):
        return pltpu.make_async_copy(h3_ref.at[t], xs_ref.at[_row_position(key_ref, starts_ref, base + t)], sem)

    def issue(blk, carry):
        for u in range(DMA_UNROLL):
            t = blk * DMA_UNROLL + u
            row_copy(t, key1_ref).start()
            row_copy(t, key2_ref).start()
        return carry

    def drain(blk, carry):
        for u in range(DMA_UNROLL):
            t = blk * DMA_UNROLL + u
            row_copy(t, key1_ref).wait()
            row_copy(t, key2_ref).wait()
        return carry

    lax.fori_loop(0, tm // DMA_UNROLL, issue, 0)
    lax.fori_loop(0, tm // DMA_UNROLL, drain, 0)


def _dispatch(key1, key2, starts, h3, n_rows, *, tm):
    T = h3.shape[0]
    xs0 = jnp.zeros((n_rows, ROW_SUB, LANES), F32)
    return pl.pallas_call(
        _dispatch_kernel,
        grid_spec=pltpu.PrefetchScalarGridSpec(
            num_scalar_prefetch=3,
            grid=(T // tm,),
            in_specs=[pl.BlockSpec((tm, ROW_SUB, LANES), lambda i, *_: (i, 0, 0)),
                      pl.BlockSpec(memory_space=pl.ANY)],
            out_specs=pl.BlockSpec(memory_space=pl.ANY),
            scratch_shapes=[pltpu.SemaphoreType.DMA],
        ),
        out_shape=jax.ShapeDtypeStruct((n_rows, ROW_SUB, LANES), F32),
        input_output_aliases={4: 0},
        compiler_params=pltpu.CompilerParams(dimension_semantics=("arbitrary",)),
        name="dispatch",
    )(key1, key2, starts, h3, xs0)


def _expert_kernel(te_ref, nu_ref, x3_ref, wg_ref, wu_ref, wd_ref, y3_ref, stage, h_scr):
    del te_ref
    j = pl.program_id(1)
    used = pl.program_id(0) < nu_ref[0]

    @pl.when(jnp.logical_not(used) & (j == 0))
    def _():
        y3_ref[...] = jnp.zeros_like(y3_ref)

    @pl.when(used)
    def _():
        @pl.when(j == 0)
        def _():
            _rows_to_stage(x3_ref, stage)
            h_scr[...] = _stage_load(stage).astype(BF16)

        h = h_scr[...]
        gate = jnp.dot(h, wg_ref[0], preferred_element_type=F32)
        up = jnp.dot(h, wu_ref[0], preferred_element_type=F32)
        act = (_silu(gate) * up).astype(BF16)
        part = jnp.dot(act, wd_ref[0], preferred_element_type=F32)

        @pl.when(j == 0)
        def _():
            _stage_store(stage, part)

        @pl.when(j > 0)
        def _():
            _stage_store(stage, part, accumulate=True)

        @pl.when(j == pl.num_programs(1) - 1)
        def _():
            _stage_to_rows(stage, y3_ref)


def _experts(tile_expert, n_used, xs3, w_gate, w_up, w_down, *, tmg, tf):
    n_rows = xs3.shape[0]
    E, D, F = w_gate.shape
    nj = F // tf
    row_blk = lambda i, j, te, nu: (jnp.minimum(i, nu[0] - 1), 0, 0)

    def chunk(i, j, nu):
        serp = lambda ii, jj: jnp.where(ii % 2 == 0, jj, nj - 1 - jj)
        return jnp.where(i < nu[0], serp(i, j), serp(nu[0] - 1, nj - 1))

    return pl.pallas_call(
        _expert_kernel,
        grid_spec=pltpu.PrefetchScalarGridSpec(
            num_scalar_prefetch=2,
            grid=(n_rows // tmg, nj),
            in_specs=[
                pl.BlockSpec((tmg, ROW_SUB, LANES), row_blk),
                pl.BlockSpec((1, D, tf), lambda i, j, te, nu: (te[i], 0, chunk(i, j, nu))),
                pl.BlockSpec((1, D, tf), lambda i, j, te, nu: (te[i], 0, chunk(i, j, nu))),
                pl.BlockSpec((1, tf, D), lambda i, j, te, nu: (te[i], chunk(i, j, nu), 0)),
            ],
            out_specs=pl.BlockSpec((tmg, ROW_SUB, LANES), lambda i, j, te, nu: (i, 0, 0)),
            scratch_shapes=[pltpu.VMEM((tmg // SUBLANES, ROW_SUB * SUBLANES, LANES), F32),
                            pltpu.VMEM((tmg, D), BF16)],
        ),
        out_shape=jax.ShapeDtypeStruct((n_rows, ROW_SUB, LANES), F32),
        compiler_params=pltpu.CompilerParams(
            dimension_semantics=("arbitrary", "arbitrary"), vmem_limit_bytes=VMEM_LIMIT),
        name="experts",
    )(tile_expert, n_used, xs3, w_gate.astype(BF16), w_up.astype(BF16), w_down.astype(BF16))


def _combine_kernel(key1_ref, key2_ref, starts_ref, x_ref, wts_ref, fg_ref, y3_ref, o_ref,
                    rows1, rows2, stage, sem):
    tm = x_ref.shape[0]
    base = pl.program_id(0) * tm

    def row_copy(t, key_ref, rows):
        return pltpu.make_async_copy(y3_ref.at[_row_position(key_ref, starts_ref, base + t)], rows.at[t], sem)

    def issue(blk, carry):
        for u in range(DMA_UNROLL):
            t = blk * DMA_UNROLL + u
            row_copy(t, key1_ref, rows1).start()
            row_copy(t, key2_ref, rows2).start()
        return carry

    def drain(blk, carry):
        for u in range(DMA_UNROLL):
            t = blk * DMA_UNROLL + u
            row_copy(t, key1_ref, rows1).wait()
            row_copy(t, key2_ref, rows2).wait()
        return carry

    lax.fori_loop(0, tm // DMA_UNROLL, issue, 0)
    lax.fori_loop(0, tm // DMA_UNROLL, drain, 0)

    w = wts_ref[...]
    _rows_to_stage(rows1, stage)
    y1 = _stage_load(stage) * w[:, 0:1]
    _rows_to_stage(rows2, stage)
    y2 = _stage_load(stage) * w[:, 1:2]
    o_ref[...] = _rms(x_ref[...] + (y1 + y2), fg_ref[...])


def _combine(key1, key2, starts, x, wts, final_g, y3, *, tm):
    T, D = x.shape
    return pl.pallas_call(
        _combine_kernel,
        grid_spec=pltpu.PrefetchScalarGridSpec(
            num_scalar_prefetch=3,
            grid=(T // tm,),
            in_specs=[
                pl.BlockSpec((tm, D), lambda i, *_: (i, 0)),
                pl.BlockSpec((tm, LANES), lambda i, *_: (i, 0)),
                pl.BlockSpec((1, D), lambda i, *_: (0, 0)),
                pl.BlockSpec(memory_space=pl.ANY),
            ],
            out_specs=pl.BlockSpec((tm, D), lambda i, *_: (i, 0)),
            scratch_shapes=[pltpu.VMEM((tm, ROW_SUB, LANES), F32),
                            pltpu.VMEM((tm, ROW_SUB, LANES), F32),
                            pltpu.VMEM((tm // SUBLANES, ROW_SUB * SUBLANES, LANES), F32),
                            pltpu.SemaphoreType.DMA],
        ),
        out_shape=jax.ShapeDtypeStruct((T, D), F32),
        compiler_params=pltpu.CompilerParams(dimension_semantics=("arbitrary",), vmem_limit_bytes=VMEM_LIMIT),
        name="combine",
    )(key1, key2, starts, x, wts, final_g.reshape(1, D), y3)


def _moe(x, g, w_router, w_gate, w_up, w_down, final_g, *, tm, tmg, tf):
    T, D = x.shape
    E = N_EXPERTS
    h3, keys, wts, cnt = _router(x, g, w_router, tm=tm)
    key1, key2 = keys[:, 0], keys[:, 1]

    counts = cnt[0, :E]
    tiles = (counts + (tmg - 1)) // tmg
    tile_end = jnp.cumsum(tiles)
    starts = ((tile_end - tiles) * tmg).astype(I32)
    n_tiles = 2 * T // tmg + E
    n_used = tile_end[-1:].astype(I32)
    last_expert = jnp.max(jnp.where(tiles > 0, jnp.arange(E), 0))
    tile_expert = jnp.sum(jnp.arange(n_tiles)[:, None] >= tile_end[None, :], axis=1)
    tile_expert = jnp.minimum(tile_expert, last_expert).astype(I32)

    xs3 = _dispatch(key1, key2, starts, h3, n_tiles * tmg, tm=tm)
    y3 = _experts(tile_expert, n_used, xs3, w_gate, w_up, w_down, tmg=tmg, tf=tf)
    return _combine(key1, key2, starts, x, wts, final_g, y3, tm=tm)


def kernel(x, attn_norm, ffn_norm, w_in, b_in, conv_w, conv_b, conv_ln_g, conv_ln_b, sinks, w_out, b_out,
           ffn_w_gate, ffn_w_up, ffn_w_down, moe_router, moe_w_gate, moe_w_up, moe_w_down, final_norm):
    B, S, D = x.shape
    ts = min(S, 512)
    tm = min(B * S, 512)

    def mixer(x, l):
        return _mixer(x, attn_norm[l], w_in[l], b_in[l], conv_w[l], conv_b[l], conv_ln_g[l], conv_ln_b[l],
                      sinks[l], w_out[l], b_out[l], ts=ts)

    x = mixer(x, 0)
    x = _ffn(x.reshape(B * S, D), ffn_norm[0], ffn_w_gate[0], ffn_w_up[0], ffn_w_down[0],
             tm=tm, tf=ffn_w_gate.shape[2] // 2)
    x = mixer(x.reshape(B, S, D), 1)
    x = _moe(x.reshape(B * S, D), ffn_norm[1], moe_router[0], moe_w_gate[0], moe_w_up[0], moe_w_down[0],
             final_norm, tm=tm, tmg=tm, tf=moe_w_gate.shape[3] // 2)
    return x.reshape(B, S, D)
```

```python
import functools

import jax
import jax.numpy as jnp
from jax import lax
from jax.experimental import pallas as pl
from jax.experimental.pallas import tpu as pltpu

F32 = jnp.float32
BF16 = jnp.bfloat16
I32 = jnp.int32

D_MODEL = 1024
CONV_CH = 512
CONV_KERNEL = 31
HEAD_DIM = 64
N_Q_HEADS = 8
N_KV_HEADS = 2
ATTN_WIDTH = N_Q_HEADS * HEAD_DIM
KV_WIDTH = N_KV_HEADS * HEAD_DIM
BLOCK = 128
N_EXPERTS = 8
EPS = 1e-5

LANES = 128
SUBLANES = 8
CONV_HALO = 32
CONV_ROWS = 32
VMEM_LIMIT = 56 * 1024 * 1024

COL_Q = 2 * CONV_CH
COL_K = COL_Q + ATTN_WIDTH
COL_V = COL_K + KV_WIDTH
IN_COLS = COL_V + KV_WIDTH

ROW_SUB = D_MODEL // LANES
KEY_SHIFT = 20
KEY_MASK = (1 << KEY_SHIFT) - 1
DMA_UNROLL = 8


def _rms(x, g):
    ms = jnp.mean(x * x, axis=-1, keepdims=True)
    return x * lax.rsqrt(ms + EPS) * g


def _silu(x):
    return x * jax.nn.sigmoid(x)


def _mixer_kernel(x_ref, g_ref, win_ref, bin_ref, cw_ref, cb_ref, lng_ref, lnb_ref,
                  sink_ref, wout_ref, bout_ref, o_ref, halo, sbuf, kbuf, vbuf, ybuf, *, ts):
    first = pl.program_id(1) == 0
    x = x_ref[0]
    h = _rms(x, g_ref[...]).astype(BF16)

    ag = jnp.dot(h, win_ref[:, 0:COL_Q], preferred_element_type=F32) + bin_ref[:, 0:COL_Q]
    u = ag[:, :CONV_CH] * jax.nn.sigmoid(ag[:, CONV_CH:])

    ucat = jnp.concatenate([jnp.where(first, 0.0, halo[...]), u], axis=0)
    halo[...] = u[ts - CONV_HALO:, :]
    for s in range(SUBLANES):
        sbuf[s, s:s + CONV_HALO + ts, :] = ucat

    tap0 = CONV_HALO - (CONV_KERNEL - 1)
    for r0 in range(0, ts, CONV_ROWS):
        acc = jnp.broadcast_to(cb_ref[...], (CONV_ROWS, CONV_CH))
        for j in range(CONV_KERNEL):
            s = -(tap0 + j) % SUBLANES
            acc = acc + cw_ref[j:j + 1, :] * sbuf[s, r0 + tap0 + j + s:r0 + tap0 + j + s + CONV_ROWS, :]
        mu = jnp.mean(acc, axis=-1, keepdims=True)
        xc = acc - mu
        var = jnp.mean(xc * xc, axis=-1, keepdims=True)
        yn = xc * lax.rsqrt(var + EPS) * lng_ref[...] + lnb_ref[...]
        ybuf[r0:r0 + CONV_ROWS, 0:CONV_CH] = _silu(yn).astype(BF16)

    q = jnp.dot(h, win_ref[:, COL_Q:COL_K], preferred_element_type=F32) + bin_ref[:, COL_Q:COL_K]
    q = (q * (HEAD_DIM ** -0.5)).astype(BF16)
    kv = jnp.dot(h, win_ref[:, COL_K:IN_COLS], preferred_element_type=F32) + bin_ref[:, COL_K:IN_COLS]

    for buf in (kbuf, vbuf):
        buf[0:BLOCK, :] = jnp.where(first, jnp.zeros((BLOCK, 4 * LANES), BF16), buf[ts:ts + BLOCK, :])

    lane = lax.broadcasted_iota(I32, (ts, LANES), 1)
    low = lane < HEAD_DIM
    for src, buf in ((kv[:, 0:LANES], kbuf), (kv[:, LANES:2 * LANES], vbuf)):
        swapped = pltpu.roll(src, HEAD_DIM, axis=1)
        zero = jnp.zeros_like(src)
        buf[BLOCK:BLOCK + ts, 0 * LANES:1 * LANES] = jnp.where(low, src, zero).astype(BF16)
        buf[BLOCK:BLOCK + ts, 1 * LANES:2 * LANES] = jnp.where(low, zero, swapped).astype(BF16)
        buf[BLOCK:BLOCK + ts, 2 * LANES:3 * LANES] = jnp.where(low, swapped, zero).astype(BF16)
        buf[BLOCK:BLOCK + ts, 3 * LANES:4 * LANES] = jnp.where(low, zero, src).astype(BF16)

    qi = lax.broadcasted_iota(I32, (BLOCK, 2 * BLOCK), 0)
    kj = lax.broadcasted_iota(I32, (BLOCK, 2 * BLOCK), 1)
    band = (kj > qi) & (kj <= qi + BLOCK)
    band_first = band & ((kj >= BLOCK) | jnp.logical_not(first))
    low_o = lax.broadcasted_iota(I32, (BLOCK, LANES), 1) < HEAD_DIM

    for n in range(ts // BLOCK):
        rows = slice(n * BLOCK, (n + 2) * BLOCK)
        mask = band_first if n == 0 else band
        for hkv in range(N_KV_HEADS):
            k_bd = jnp.concatenate([kbuf[rows, (2 * hkv) * LANES:(2 * hkv + 1) * LANES],
                                    kbuf[rows, (2 * hkv + 1) * LANES:(2 * hkv + 2) * LANES]], axis=0)
            v_bd = jnp.concatenate([vbuf[rows, (2 * hkv) * LANES:(2 * hkv + 1) * LANES],
                                    vbuf[rows, (2 * hkv + 1) * LANES:(2 * hkv + 2) * LANES]], axis=0)
            for pair in range(2):
                hp = 2 * hkv + pair
                qp = q[n * BLOCK:(n + 1) * BLOCK, hp * LANES:(hp + 1) * LANES]
                s = lax.dot_general(qp, k_bd, (((1,), (1,)), ((), ())),
                                    preferred_element_type=F32)
                ps, rden = [], []
                for hh in range(2):
                    sink = sink_ref[2 * hp + hh]
                    sh = jnp.where(mask, s[:, hh * 2 * BLOCK:(hh + 1) * 2 * BLOCK], -jnp.inf)
                    m = jnp.maximum(jnp.max(sh, axis=-1, keepdims=True), sink)
                    p = jnp.exp(sh - m)
                    den = jnp.sum(p, axis=-1, keepdims=True) + jnp.exp(sink - m)
                    ps.append(p.astype(BF16))
                    rden.append(1.0 / den)
                o = jnp.dot(jnp.concatenate(ps, axis=1), v_bd, preferred_element_type=F32)
                o = o * jnp.where(low_o, rden[0], rden[1])
                ybuf[n * BLOCK:(n + 1) * BLOCK, CONV_CH + hp * LANES:CONV_CH + (hp + 1) * LANES] = o.astype(BF16)

    y = jnp.dot(ybuf[...], wout_ref[...], preferred_element_type=F32)
    o_ref[0] = x + y + bout_ref[...]


def _mixer(x, g, w_in, b_in, conv_w, conv_b, ln_g, ln_b, sinks, w_out, b_out, *, ts):
    B, S, D = x.shape
    row = lambda a: a.reshape(1, -1)
    const = lambda shape: pl.BlockSpec(shape, lambda b, s: (0,) * len(shape))
    return pl.pallas_call(
        functools.partial(_mixer_kernel, ts=ts),
        grid=(B, S // ts),
        in_specs=[
            pl.BlockSpec((1, ts, D), lambda b, s: (b, s, 0)),
            const((1, D)),
            const((D, IN_COLS)),
            const((1, IN_COLS)),
            const((CONV_KERNEL, CONV_CH)),
            const((1, CONV_CH)),
            const((1, CONV_CH)),
            const((1, CONV_CH)),
            pl.BlockSpec(memory_space=pltpu.SMEM),
            const((D, D)),
            const((1, D)),
        ],
        out_specs=pl.BlockSpec((1, ts, D), lambda b, s: (b, s, 0)),
        out_shape=jax.ShapeDtypeStruct((B, S, D), F32),
        scratch_shapes=[
            pltpu.VMEM((CONV_HALO, CONV_CH), F32),
            pltpu.VMEM((SUBLANES, CONV_HALO + ts + SUBLANES, CONV_CH), F32),
            pltpu.VMEM((BLOCK + ts, 4 * LANES), BF16),
            pltpu.VMEM((BLOCK + ts, 4 * LANES), BF16),
            pltpu.VMEM((ts, D), BF16),
        ],
        compiler_params=pltpu.CompilerParams(
            dimension_semantics=("arbitrary", "arbitrary"), vmem_limit_bytes=VMEM_LIMIT),
        name="mixer",
    )(x, row(g), w_in.astype(BF16), row(b_in), conv_w, row(conv_b), row(ln_g), row(ln_b),
      sinks, w_out.astype(BF16), row(b_out))


def _ffn_kernel(x_ref, g_ref, wg_ref, wu_ref, wd_ref, o_ref, h_scr, acc_scr):
    j = pl.program_id(1)

    @pl.when(j == 0)
    def _():
        x = x_ref[...]
        h_scr[...] = _rms(x, g_ref[...]).astype(BF16)
        acc_scr[...] = x

    h = h_scr[...]
    gate = jnp.dot(h, wg_ref[...], preferred_element_type=F32)
    up = jnp.dot(h, wu_ref[...], preferred_element_type=F32)
    a = (_silu(gate) * up).astype(BF16)
    acc_scr[...] += jnp.dot(a, wd_ref[...], preferred_element_type=F32)

    @pl.when(j == pl.num_programs(1) - 1)
    def _():
        o_ref[...] = acc_scr[...]


def _ffn(x, g, w_gate, w_up, w_down, *, tm, tf):
    T, D = x.shape
    F = w_gate.shape[1]
    return pl.pallas_call(
        _ffn_kernel,
        grid=(T // tm, F // tf),
        in_specs=[
            pl.BlockSpec((tm, D), lambda i, j: (i, 0)),
            pl.BlockSpec((1, D), lambda i, j: (0, 0)),
            pl.BlockSpec((D, tf), lambda i, j: (0, j)),
            pl.BlockSpec((D, tf), lambda i, j: (0, j)),
            pl.BlockSpec((tf, D), lambda i, j: (j, 0)),
        ],
        out_specs=pl.BlockSpec((tm, D), lambda i, j: (i, 0)),
        out_shape=jax.ShapeDtypeStruct((T, D), F32),
        scratch_shapes=[pltpu.VMEM((tm, D), BF16), pltpu.VMEM((tm, D), F32)],
        compiler_params=pltpu.CompilerParams(
            dimension_semantics=("arbitrary", "arbitrary"), vmem_limit_bytes=VMEM_LIMIT),
        name="dense_ffn",
    )(x, g.reshape(1, D), w_gate.astype(BF16), w_up.astype(BF16), w_down.astype(BF16))


def _stage_store(stage_ref, val, accumulate=False):
    n = val.shape[0]
    for c in range(ROW_SUB):
        tile = val[:, c * LANES:(c + 1) * LANES].reshape(n // SUBLANES, SUBLANES, LANES)
        if accumulate:
            stage_ref[:, c * SUBLANES:(c + 1) * SUBLANES, :] += tile
        else:
            stage_ref[:, c * SUBLANES:(c + 1) * SUBLANES, :] = tile


def _stage_load(stage_ref):
    n = stage_ref.shape[0] * SUBLANES
    return jnp.concatenate(
        [stage_ref[:, c * SUBLANES:(c + 1) * SUBLANES, :].reshape(n, LANES) for c in range(ROW_SUB)], axis=1)


def _stage_to_rows(stage_ref, rows_ref):
    def body(g, carry):
        for t in range(SUBLANES):
            rows_ref[g * SUBLANES + t] = stage_ref[g, pl.ds(t, ROW_SUB, stride=SUBLANES), :]
        return carry

    lax.fori_loop(0, stage_ref.shape[0], body, 0)


def _rows_to_stage(rows_ref, stage_ref):
    def body(g, carry):
        for t in range(SUBLANES):
            stage_ref[g, pl.ds(t, ROW_SUB, stride=SUBLANES), :] = rows_ref[g * SUBLANES + t]
        return carry

    lax.fori_loop(0, stage_ref.shape[0], body, 0)


def _split_bf16(a):
    hi = a.astype(BF16)
    lo = (a - hi.astype(F32)).astype(BF16)
    return hi, lo


def _router_kernel(x_ref, g_ref, wr_hi_ref, wr_lo_ref, h3_ref, keys_ref, wts_ref, cnt_ref, stage, count):
    tm = x_ref.shape[0]
    lane = lax.broadcasted_iota(I32, (tm, LANES), 1)

    @pl.when(pl.program_id(0) == 0)
    def _():
        count[...] = jnp.zeros_like(count)

    hf = _rms(x_ref[...], g_ref[...])
    h_hi, h_lo = _split_bf16(hf)
    logits = (jnp.dot(h_hi, wr_hi_ref[...], preferred_element_type=F32)
              + jnp.dot(h_lo, wr_hi_ref[...], preferred_element_type=F32)
              + jnp.dot(h_hi, wr_lo_ref[...], preferred_element_type=F32))
    lg = jnp.where(lane < N_EXPERTS, logits, -jnp.inf)
    v1 = jnp.max(lg, axis=-1, keepdims=True)
    i1 = jnp.min(jnp.where(lg == v1, lane, LANES), axis=-1, keepdims=True)
    lg2 = jnp.where(lane == i1, -jnp.inf, lg)
    v2 = jnp.max(lg2, axis=-1, keepdims=True)
    i2 = jnp.min(jnp.where(lg2 == v2, lane, LANES), axis=-1, keepdims=True)
    t = jnp.exp(v2 - v1)
    w1 = 1.0 / (1.0 + t)
    w2 = t / (1.0 + t)

    onehot = jnp.where((lane == i1) | (lane == i2), 1.0, 0.0)
    r = lax.broadcasted_iota(I32, (tm, tm), 0)
    c = lax.broadcasted_iota(I32, (tm, tm), 1)
    earlier = jnp.where(c < r, 1.0, 0.0).astype(BF16)
    before = count[...] + jnp.dot(earlier, onehot.astype(BF16), preferred_element_type=F32)
    rank1 = jnp.sum(jnp.where(lane == i1, before, 0.0), axis=-1, keepdims=True).astype(I32)
    rank2 = jnp.sum(jnp.where(lane == i2, before, 0.0), axis=-1, keepdims=True).astype(I32)
    count[...] += jnp.sum(onehot, axis=0, keepdims=True)

    key1 = i1 * (1 << KEY_SHIFT) + rank1
    key2 = i2 * (1 << KEY_SHIFT) + rank2
    keys_ref[...] = jnp.where(lane == 0, key1, jnp.where(lane == 1, key2, 0))
    wts_ref[...] = jnp.where(lane == 0, w1, jnp.where(lane == 1, w2, 0.0))
    cnt_ref[...] = count[...].astype(I32)
    _stage_store(stage, hf)
    _stage_to_rows(stage, h3_ref)


def _router(x, g, w_router, *, tm):
    T, D = x.shape
    wr = jnp.pad(w_router, ((0, 0), (0, LANES - N_EXPERTS)))
    wr_hi = wr.astype(BF16)
    wr_lo = (wr - wr_hi.astype(F32)).astype(BF16)
    return pl.pallas_call(
        _router_kernel,
        grid=(T // tm,),
        in_specs=[
            pl.BlockSpec((tm, D), lambda i: (i, 0)),
            pl.BlockSpec((1, D), lambda i: (0, 0)),
            pl.BlockSpec((D, LANES), lambda i: (0, 0)),
            pl.BlockSpec((D, LANES), lambda i: (0, 0)),
        ],
        out_specs=[
            pl.BlockSpec((tm, ROW_SUB, LANES), lambda i: (i, 0, 0)),
            pl.BlockSpec((tm, LANES), lambda i: (i, 0)),
            pl.BlockSpec((tm, LANES), lambda i: (i, 0)),
            pl.BlockSpec((1, LANES), lambda i: (0, 0)),
        ],
        out_shape=[
            jax.ShapeDtypeStruct((T, ROW_SUB, LANES), F32),
            jax.ShapeDtypeStruct((T, LANES), I32),
            jax.ShapeDtypeStruct((T, LANES), F32),
            jax.ShapeDtypeStruct((1, LANES), I32),
        ],
        scratch_shapes=[pltpu.VMEM((tm // SUBLANES, ROW_SUB * SUBLANES, LANES), F32),
                        pltpu.VMEM((1, LANES), F32)],
        compiler_params=pltpu.CompilerParams(dimension_semantics=("arbitrary",), vmem_limit_bytes=VMEM_LIMIT),
        name="router",
    )(x, g.reshape(1, D), wr_hi, wr_lo)


def _dispatch_kernel(pos1_ref, pos2_ref, h3_ref, xs_in_ref, xs_ref, sem):
    del xs_in_ref
    tm = h3_ref.shape[0]
    base = pl.program_id(0) * tm

    def issue(blk, carry):
        for u in range(DMA_UNROLL):
            t = blk * DMA_UNROLL + u
            for pos_ref in (pos1_ref, pos2_ref):
                pltpu.make_async_copy(h3_ref.at[t], xs_ref.at[pos_ref[base + t]], sem).start()
        return carry

    lax.fori_loop(0, tm // DMA_UNROLL, issue, 0)
    for _ in range(2):
        pltpu.make_async_copy(h3_ref, xs_ref.at[pl.ds(0, tm)], sem).wait()


def _dispatch(pos1, pos2, h3, n_rows, *, tm):
    T = h3.shape[0]
    xs0 = jnp.zeros((n_rows, ROW_SUB, LANES), F32)
    return pl.pallas_call(
        _dispatch_kernel,
        grid_spec=pltpu.PrefetchScalarGridSpec(
            num_scalar_prefetch=2,
            grid=(T // tm,),
            in_specs=[pl.BlockSpec((tm, ROW_SUB, LANES), lambda i, *_: (i, 0, 0)),
                      pl.BlockSpec(memory_space=pl.ANY)],
            out_specs=pl.BlockSpec(memory_space=pl.ANY),
            scratch_shapes=[pltpu.SemaphoreType.DMA],
        ),
        out_shape=jax.ShapeDtypeStruct((n_rows, ROW_SUB, LANES), F32),
        input_output_aliases={3: 0},
        compiler_params=pltpu.CompilerParams(dimension_semantics=("arbitrary",)),
        name="dispatch",
    )(pos1, pos2, h3, xs0)


def _expert_kernel(te_ref, nu_ref, x3_ref, wg_ref, wu_ref, wd_ref, y3_ref, stage, h_scr):
    del te_ref
    j = pl.program_id(1)
    used = pl.program_id(0) < nu_ref[0]

    @pl.when(jnp.logical_not(used) & (j == 0))
    def _():
        y3_ref[...] = jnp.zeros_like(y3_ref)

    @pl.when(used)
    def _():
        @pl.when(j == 0)
        def _():
            _rows_to_stage(x3_ref, stage)
            h_scr[...] = _stage_load(stage).astype(BF16)

        h = h_scr[...]
        gate = jnp.dot(h, wg_ref[0], preferred_element_type=F32)
        up = jnp.dot(h, wu_ref[0], preferred_element_type=F32)
        act = (_silu(gate) * up).astype(BF16)
        part = jnp.dot(act, wd_ref[0], preferred_element_type=F32)

        @pl.when(j == 0)
        def _():
            _stage_store(stage, part)

        @pl.when(j > 0)
        def _():
            _stage_store(stage, part, accumulate=True)

        @pl.when(j == pl.num_programs(1) - 1)
        def _():
            _stage_to_rows(stage, y3_ref)


def _experts(tile_expert, n_used, xs3, w_gate, w_up, w_down, *, tmg, tf):
    n_rows = xs3.shape[0]
    E, D, F = w_gate.shape
    nj = F // tf
    row_blk = lambda i, j, te, nu: (jnp.minimum(i, nu[0] - 1), 0, 0)

    def chunk(i, j, nu):
        serp = lambda ii, jj: jnp.where(ii % 2 == 0, jj, nj - 1 - jj)
        return jnp.where(i < nu[0], serp(i, j), serp(nu[0] - 1, nj - 1))

    return pl.pallas_call(
        _expert_kernel,
        grid_spec=pltpu.PrefetchScalarGridSpec(
            num_scalar_prefetch=2,
            grid=(n_rows // tmg, nj),
            in_specs=[
                pl.BlockSpec((tmg, ROW_SUB, LANES), row_blk),
                pl.BlockSpec((1, D, tf), lambda i, j, te, nu: (te[i], 0, chunk(i, j, nu))),
                pl.BlockSpec((1, D, tf), lambda i, j, te, nu: (te[i], 0, chunk(i, j, nu))),
                pl.BlockSpec((1, tf, D), lambda i, j, te, nu: (te[i], chunk(i, j, nu), 0)),
            ],
            out_specs=pl.BlockSpec((tmg, ROW_SUB, LANES), lambda i, j, te, nu: (i, 0, 0)),
            scratch_shapes=[pltpu.VMEM((tmg // SUBLANES, ROW_SUB * SUBLANES, LANES), F32),
                            pltpu.VMEM((tmg, D), BF16)],
        ),
        out_shape=jax.ShapeDtypeStruct((n_rows, ROW_SUB, LANES), F32),
        compiler_params=pltpu.CompilerParams(
            dimension_semantics=("arbitrary", "arbitrary"), vmem_limit_bytes=VMEM_LIMIT),
        name="experts",
    )(tile_expert, n_used, xs3, w_gate.astype(BF16), w_up.astype(BF16), w_down.astype(BF16))


def _combine_kernel(pos1_ref, pos2_ref, x_ref, wts_ref, fg_ref, y3_ref, o_ref, rows, stage, sems):
    tm = x_ref.shape[0]
    i = pl.program_id(0)

    def issue(tile, slot):
        base = tile * tm

        def body(blk, carry):
            for u in range(DMA_UNROLL):
                t = blk * DMA_UNROLL + u
                for k, pos_ref in enumerate((pos1_ref, pos2_ref)):
                    pltpu.make_async_copy(y3_ref.at[pos_ref[base + t]], rows.at[slot, k, t], sems.at[slot]).start()
            return carry

        lax.fori_loop(0, tm // DMA_UNROLL, body, 0)

    @pl.when(i == 0)
    def _():
        issue(0, 0)

    @pl.when(i + 1 < pl.num_programs(0))
    def _():
        issue(i + 1, (i + 1) % 2)

    slot = i % 2
    for k in range(2):
        pltpu.make_async_copy(y3_ref.at[pl.ds(0, tm)], rows.at[slot, k], sems.at[slot]).wait()

    w = wts_ref[...]
    _rows_to_stage(rows.at[slot, 0], stage)
    y1 = _stage_load(stage) * w[:, 0:1]
    _rows_to_stage(rows.at[slot, 1], stage)
    y2 = _stage_load(stage) * w[:, 1:2]
    o_ref[...] = _rms(x_ref[...] + (y1 + y2), fg_ref[...])


def _combine(pos1, pos2, x, wts, final_g, y3, *, tm):
    T, D = x.shape
    return pl.pallas_call(
        _combine_kernel,
        grid_spec=pltpu.PrefetchScalarGridSpec(
            num_scalar_prefetch=2,
            grid=(T // tm,),
            in_specs=[
                pl.BlockSpec((tm, D), lambda i, *_: (i, 0)),
                pl.BlockSpec((tm, LANES), lambda i, *_: (i, 0)),
                pl.BlockSpec((1, D), lambda i, *_: (0, 0)),
                pl.BlockSpec(memory_space=pl.ANY),
            ],
            out_specs=pl.BlockSpec((tm, D), lambda i, *_: (i, 0)),
            scratch_shapes=[pltpu.VMEM((2, 2, tm, ROW_SUB, LANES), F32),
                            pltpu.VMEM((tm // SUBLANES, ROW_SUB * SUBLANES, LANES), F32),
                            pltpu.SemaphoreType.DMA((2,))],
        ),
        out_shape=jax.ShapeDtypeStruct((T, D), F32),
        compiler_params=pltpu.CompilerParams(dimension_semantics=("arbitrary",), vmem_limit_bytes=VMEM_LIMIT),
        name="combine",
    )(pos1, pos2, x, wts, final_g.reshape(1, D), y3)


def _moe(x, g, w_router, w_gate, w_up, w_down, final_g, *, tm, tmg, tf):
    T, D = x.shape
    E = N_EXPERTS
    h3, keys, wts, cnt = _router(x, g, w_router, tm=tm)
    key1, key2 = keys[:, 0], keys[:, 1]

    counts = cnt[0, :E]
    tiles = (counts + (tmg - 1)) // tmg
    tile_end = jnp.cumsum(tiles)
    starts = ((tile_end - tiles) * tmg).astype(I32)
    n_tiles = 2 * T // tmg + E
    n_used = tile_end[-1:].astype(I32)
    last_expert = jnp.max(jnp.where(tiles > 0, jnp.arange(E), 0))
    tile_expert = jnp.sum(jnp.arange(n_tiles)[:, None] >= tile_end[None, :], axis=1)
    tile_expert = jnp.minimum(tile_expert, last_expert).astype(I32)

    pos1 = starts[key1 >> KEY_SHIFT] + (key1 & KEY_MASK)
    pos2 = starts[key2 >> KEY_SHIFT] + (key2 & KEY_MASK)

    xs3 = _dispatch(pos1, pos2, h3, n_tiles * tmg, tm=tm)
    y3 = _experts(tile_expert, n_used, xs3, w_gate, w_up, w_down, tmg=tmg, tf=tf)
    return _combine(pos1, pos2, x, wts, final_g, y3, tm=tm)


def kernel(x, attn_norm, ffn_norm, w_in, b_in, conv_w, conv_b, conv_ln_g, conv_ln_b, sinks, w_out, b_out,
           ffn_w_gate, ffn_w_up, ffn_w_down, moe_router, moe_w_gate, moe_w_up, moe_w_down, final_norm):
    B, S, D = x.shape
    ts = min(S, 512)
    tm = min(B * S, 512)

    def mixer(x, l):
        return _mixer(x, attn_norm[l], w_in[l], b_in[l], conv_w[l], conv_b[l], conv_ln_g[l], conv_ln_b[l],
                      sinks[l], w_out[l], b_out[l], ts=ts)

    x = mixer(x, 0)
    x = _ffn(x.reshape(B * S, D), ffn_norm[0], ffn_w_gate[0], ffn_w_up[0], ffn_w_down[0],
             tm=tm, tf=ffn_w_gate.shape[2] // 2)
    x = mixer(x.reshape(B, S, D), 1)
    x = _moe(x.reshape(B * S, D), ffn_norm[1], moe_router[0], moe_w_gate[0], moe_w_up[0], moe_w_down[0],
             final_norm, tm=tm, tmg=tm, tf=moe_w_gate.shape[3] // 2)
    return x.reshape(B, S, D)
```

```python
import functools

import jax
import jax.numpy as jnp
from jax import lax
from jax.experimental import pallas as pl
from jax.experimental.pallas import tpu as pltpu

F32 = jnp.float32
BF16 = jnp.bfloat16
I32 = jnp.int32

D_MODEL = 1024
CONV_CH = 512
CONV_KERNEL = 31
HEAD_DIM = 64
N_Q_HEADS = 8
N_KV_HEADS = 2
ATTN_WIDTH = N_Q_HEADS * HEAD_DIM
KV_WIDTH = N_KV_HEADS * HEAD_DIM
BLOCK = 128
N_EXPERTS = 8
EPS = 1e-5

LANES = 128
SUBLANES = 8
CONV_HALO = 32
CONV_ROWS = 32
VMEM_LIMIT = 56 * 1024 * 1024

COL_Q = 2 * CONV_CH
COL_K = COL_Q + ATTN_WIDTH
COL_V = COL_K + KV_WIDTH
IN_COLS = COL_V + KV_WIDTH

ROW_SUB = D_MODEL // LANES
KEY_SHIFT = 20
KEY_MASK = (1 << KEY_SHIFT) - 1
DMA_UNROLL = 8


def _rms(x, g):
    ms = jnp.mean(x * x, axis=-1, keepdims=True)
    return x * lax.rsqrt(ms + EPS) * g


def _silu(x):
    return x * jax.nn.sigmoid(x)


def _mixer_kernel(x_ref, g_ref, win_ref, bin_ref, cw_ref, cb_ref, lng_ref, lnb_ref,
                  sink_ref, wout_ref, bout_ref, o_ref, halo, sbuf, kbuf, vbuf, ybuf, *, ts):
    first = pl.program_id(1) == 0
    x = x_ref[0]
    h = _rms(x, g_ref[...]).astype(BF16)

    ag = jnp.dot(h, win_ref[:, 0:COL_Q], preferred_element_type=F32) + bin_ref[:, 0:COL_Q]
    u = ag[:, :CONV_CH] * jax.nn.sigmoid(ag[:, CONV_CH:])

    ucat = jnp.concatenate([jnp.where(first, 0.0, halo[...]), u], axis=0)
    halo[...] = u[ts - CONV_HALO:, :]
    for s in range(SUBLANES):
        sbuf[s, s:s + CONV_HALO + ts, :] = ucat

    tap0 = CONV_HALO - (CONV_KERNEL - 1)
    for r0 in range(0, ts, CONV_ROWS):
        acc = jnp.broadcast_to(cb_ref[...], (CONV_ROWS, CONV_CH))
        for j in range(CONV_KERNEL):
            s = -(tap0 + j) % SUBLANES
            acc = acc + cw_ref[j:j + 1, :] * sbuf[s, r0 + tap0 + j + s:r0 + tap0 + j + s + CONV_ROWS, :]
        mu = jnp.mean(acc, axis=-1, keepdims=True)
        xc = acc - mu
        var = jnp.mean(xc * xc, axis=-1, keepdims=True)
        yn = xc * lax.rsqrt(var + EPS) * lng_ref[...] + lnb_ref[...]
        ybuf[r0:r0 + CONV_ROWS, 0:CONV_CH] = _silu(yn).astype(BF16)

    q = jnp.dot(h, win_ref[:, COL_Q:COL_K], preferred_element_type=F32) + bin_ref[:, COL_Q:COL_K]
    q = (q * (HEAD_DIM ** -0.5)).astype(BF16)
    kv = jnp.dot(h, win_ref[:, COL_K:IN_COLS], preferred_element_type=F32) + bin_ref[:, COL_K:IN_COLS]

    for buf in (kbuf, vbuf):
        buf[0:BLOCK, :] = jnp.where(first, jnp.zeros((BLOCK, 4 * LANES), BF16), buf[ts:ts + BLOCK, :])

    lane = lax.broadcasted_iota(I32, (ts, LANES), 1)
    low = lane < HEAD_DIM
    for src, buf in ((kv[:, 0:LANES], kbuf), (kv[:, LANES:2 * LANES], vbuf)):
        swapped = pltpu.roll(src, HEAD_DIM, axis=1)
        zero = jnp.zeros_like(src)
        buf[BLOCK:BLOCK + ts, 0 * LANES:1 * LANES] = jnp.where(low, src, zero).astype(BF16)
        buf[BLOCK:BLOCK + ts, 1 * LANES:2 * LANES] = jnp.where(low, zero, swapped).astype(BF16)
        buf[BLOCK:BLOCK + ts, 2 * LANES:3 * LANES] = jnp.where(low, swapped, zero).astype(BF16)
        buf[BLOCK:BLOCK + ts, 3 * LANES:4 * LANES] = jnp.where(low, zero, src).astype(BF16)

    qi = lax.broadcasted_iota(I32, (BLOCK, 2 * BLOCK), 0)
    kj = lax.broadcasted_iota(I32, (BLOCK, 2 * BLOCK), 1)
    band = (kj > qi) & (kj <= qi + BLOCK)
    band_first = band & ((kj >= BLOCK) | jnp.logical_not(first))
    low_o = lax.broadcasted_iota(I32, (BLOCK, LANES), 1) < HEAD_DIM

    for n in range(ts // BLOCK):
        rows = slice(n * BLOCK, (n + 2) * BLOCK)
        mask = band_first if n == 0 else band
        for hkv in range(N_KV_HEADS):
            k_bd = jnp.concatenate([kbuf[rows, (2 * hkv) * LANES:(2 * hkv + 1) * LANES],
                                    kbuf[rows, (2 * hkv + 1) * LANES:(2 * hkv + 2) * LANES]], axis=0)
            v_bd = jnp.concatenate([vbuf[rows, (2 * hkv) * LANES:(2 * hkv + 1) * LANES],
                                    vbuf[rows, (2 * hkv + 1) * LANES:(2 * hkv + 2) * LANES]], axis=0)
            for pair in range(2):
                hp = 2 * hkv + pair
                qp = q[n * BLOCK:(n + 1) * BLOCK, hp * LANES:(hp + 1) * LANES]
                s = lax.dot_general(qp, k_bd, (((1,), (1,)), ((), ())),
                                    preferred_element_type=F32)
                ps, rden = [], []
                for hh in range(2):
                    sink = sink_ref[2 * hp + hh]
                    sh = jnp.where(mask, s[:, hh * 2 * BLOCK:(hh + 1) * 2 * BLOCK], -jnp.inf)
                    m = jnp.maximum(jnp.max(sh, axis=-1, keepdims=True), sink)
                    p = jnp.exp(sh - m)
                    den = jnp.sum(p, axis=-1, keepdims=True) + jnp.exp(sink - m)
                    ps.append(p.astype(BF16))
                    rden.append(1.0 / den)
                o = jnp.dot(jnp.concatenate(ps, axis=1), v_bd, preferred_element_type=F32)
                o = o * jnp.where(low_o, rden[0], rden[1])
                ybuf[n * BLOCK:(n + 1) * BLOCK, CONV_CH + hp * LANES:CONV_CH + (hp + 1) * LANES] = o.astype(BF16)

    y = jnp.dot(ybuf[...], wout_ref[...], preferred_element_type=F32)
    o_ref[0] = x + y + bout_ref[...]


def _mixer(x, g, w_in, b_in, conv_w, conv_b, ln_g, ln_b, sinks, w_out, b_out, *, ts):
    B, S, D = x.shape
    row = lambda a: a.reshape(1, -1)
    const = lambda shape: pl.BlockSpec(shape, lambda b, s: (0,) * len(shape))
    return pl.pallas_call(
        functools.partial(_mixer_kernel, ts=ts),
        grid=(B, S // ts),
        in_specs=[
            pl.BlockSpec((1, ts, D), lambda b, s: (b, s, 0)),
            const((1, D)),
            const((D, IN_COLS)),
            const((1, IN_COLS)),
            const((CONV_KERNEL, CONV_CH)),
            const((1, CONV_CH)),
            const((1, CONV_CH)),
            const((1, CONV_CH)),
            pl.BlockSpec(memory_space=pltpu.SMEM),
            const((D, D)),
            const((1, D)),
        ],
        out_specs=pl.BlockSpec((1, ts, D), lambda b, s: (b, s, 0)),
        out_shape=jax.ShapeDtypeStruct((B, S, D), F32),
        scratch_shapes=[
            pltpu.VMEM((CONV_HALO, CONV_CH), F32),
            pltpu.VMEM((SUBLANES, CONV_HALO + ts + SUBLANES, CONV_CH), F32),
            pltpu.VMEM((BLOCK + ts, 4 * LANES), BF16),
            pltpu.VMEM((BLOCK + ts, 4 * LANES), BF16),
            pltpu.VMEM((ts, D), BF16),
        ],
        compiler_params=pltpu.CompilerParams(
            dimension_semantics=("arbitrary", "arbitrary"), vmem_limit_bytes=VMEM_LIMIT),
        name="mixer",
    )(x, row(g), w_in.astype(BF16), row(b_in), conv_w, row(conv_b), row(ln_g), row(ln_b),
      sinks, w_out.astype(BF16), row(b_out))


def _ffn_kernel(x_ref, g_ref, wg_ref, wu_ref, wd_ref, o_ref):
    x = x_ref[...]
    h = _rms(x, g_ref[...]).astype(BF16)
    gate = jnp.dot(h, wg_ref[...], preferred_element_type=F32)
    up = jnp.dot(h, wu_ref[...], preferred_element_type=F32)
    a = (_silu(gate) * up).astype(BF16)
    o_ref[...] = x + jnp.dot(a, wd_ref[...], preferred_element_type=F32)


def _ffn(x, g, w_gate, w_up, w_down, *, tm):
    T, D = x.shape
    F = w_gate.shape[1]
    resident = lambda shape: pl.BlockSpec(shape, lambda i: (0, 0), pipeline_mode=pl.Buffered(1))
    return pl.pallas_call(
        _ffn_kernel,
        grid=(T // tm,),
        in_specs=[
            pl.BlockSpec((tm, D), lambda i: (i, 0)),
            pl.BlockSpec((1, D), lambda i: (0, 0)),
            resident((D, F)),
            resident((D, F)),
            resident((F, D)),
        ],
        out_specs=pl.BlockSpec((tm, D), lambda i: (i, 0)),
        out_shape=jax.ShapeDtypeStruct((T, D), F32),
        compiler_params=pltpu.CompilerParams(dimension_semantics=("arbitrary",), vmem_limit_bytes=VMEM_LIMIT),
        name="dense_ffn",
    )(x, g.reshape(1, D), w_gate.astype(BF16), w_up.astype(BF16), w_down.astype(BF16))


def _stage_store(stage_ref, val, accumulate=False):
    n = val.shape[0]
    for c in range(ROW_SUB):
        tile = val[:, c * LANES:(c + 1) * LANES].reshape(n // SUBLANES, SUBLANES, LANES)
        if accumulate:
            stage_ref[:, c * SUBLANES:(c + 1) * SUBLANES, :] += tile
        else:
            stage_ref[:, c * SUBLANES:(c + 1) * SUBLANES, :] = tile


def _stage_load(stage_ref):
    n = stage_ref.shape[0] * SUBLANES
    return jnp.concatenate(
        [stage_ref[:, c * SUBLANES:(c + 1) * SUBLANES, :].reshape(n, LANES) for c in range(ROW_SUB)], axis=1)


def _stage_to_rows(stage_ref, rows_ref):
    for g in range(stage_ref.shape[0]):
        for t in range(SUBLANES):
            rows_ref[g * SUBLANES + t] = stage_ref[g, pl.ds(t, ROW_SUB, stride=SUBLANES), :]


def _rows_to_stage(rows_ref, stage_ref):
    for g in range(stage_ref.shape[0]):
        for t in range(SUBLANES):
            stage_ref[g, pl.ds(t, ROW_SUB, stride=SUBLANES), :] = rows_ref[g * SUBLANES + t]


def _split_bf16(a):
    hi = a.astype(BF16)
    lo = (a - hi.astype(F32)).astype(BF16)
    return hi, lo


def _router_kernel(x_ref, g_ref, wr_hi_ref, wr_lo_ref, h3_ref, keys_ref, wts_ref, cnt_ref, stage, count):
    tm = x_ref.shape[0]
    lane = lax.broadcasted_iota(I32, (tm, LANES), 1)

    @pl.when(pl.program_id(0) == 0)
    def _():
        count[...] = jnp.zeros_like(count)

    hf = _rms(x_ref[...], g_ref[...])
    h_hi, h_lo = _split_bf16(hf)
    logits = (jnp.dot(h_hi, wr_hi_ref[...], preferred_element_type=F32)
              + jnp.dot(h_lo, wr_hi_ref[...], preferred_element_type=F32)
              + jnp.dot(h_hi, wr_lo_ref[...], preferred_element_type=F32))
    lg = jnp.where(lane < N_EXPERTS, logits, -jnp.inf)
    v1 = jnp.max(lg, axis=-1, keepdims=True)
    i1 = jnp.min(jnp.where(lg == v1, lane, LANES), axis=-1, keepdims=True)
    lg2 = jnp.where(lane == i1, -jnp.inf, lg)
    v2 = jnp.max(lg2, axis=-1, keepdims=True)
    i2 = jnp.min(jnp.where(lg2 == v2, lane, LANES), axis=-1, keepdims=True)
    t = jnp.exp(v2 - v1)
    w1 = 1.0 / (1.0 + t)
    w2 = t / (1.0 + t)

    onehot = jnp.where((lane == i1) | (lane == i2), 1.0, 0.0)
    r = lax.broadcasted_iota(I32, (tm, tm), 0)
    c = lax.broadcasted_iota(I32, (tm, tm), 1)
    earlier = jnp.where(c < r, 1.0, 0.0).astype(BF16)
    before = count[...] + jnp.dot(earlier, onehot.astype(BF16), preferred_element_type=F32)
    rank1 = jnp.sum(jnp.where(lane == i1, before, 0.0), axis=-1, keepdims=True).astype(I32)
    rank2 = jnp.sum(jnp.where(lane == i2, before, 0.0), axis=-1, keepdims=True).astype(I32)
    count[...] += jnp.sum(onehot, axis=0, keepdims=True)

    key1 = i1 * (1 << KEY_SHIFT) + rank1
    key2 = i2 * (1 << KEY_SHIFT) + rank2
    keys_ref[...] = jnp.where(lane == 0, key1, jnp.where(lane == 1, key2, 0))
    wts_ref[...] = jnp.where(lane == 0, w1, jnp.where(lane == 1, w2, 0.0))
    cnt_ref[...] = count[...].astype(I32)
    _stage_store(stage, hf)
    _stage_to_rows(stage, h3_ref)


def _router(x, g, w_router, *, tm):
    T, D = x.shape
    wr = jnp.pad(w_router, ((0, 0), (0, LANES - N_EXPERTS)))
    wr_hi = wr.astype(BF16)
    wr_lo = (wr - wr_hi.astype(F32)).astype(BF16)
    return pl.pallas_call(
        _router_kernel,
        grid=(T // tm,),
        in_specs=[
            pl.BlockSpec((tm, D), lambda i: (i, 0)),
            pl.BlockSpec((1, D), lambda i: (0, 0)),
            pl.BlockSpec((D, LANES), lambda i: (0, 0)),
            pl.BlockSpec((D, LANES), lambda i: (0, 0)),
        ],
        out_specs=[
            pl.BlockSpec((tm, ROW_SUB, LANES), lambda i: (i, 0, 0)),
            pl.BlockSpec((tm, LANES), lambda i: (i, 0)),
            pl.BlockSpec((tm, LANES), lambda i: (i, 0)),
            pl.BlockSpec((1, LANES), lambda i: (0, 0)),
        ],
        out_shape=[
            jax.ShapeDtypeStruct((T, ROW_SUB, LANES), F32),
            jax.ShapeDtypeStruct((T, LANES), I32),
            jax.ShapeDtypeStruct((T, LANES), F32),
            jax.ShapeDtypeStruct((1, LANES), I32),
        ],
        scratch_shapes=[pltpu.VMEM((tm // SUBLANES, ROW_SUB * SUBLANES, LANES), F32),
                        pltpu.VMEM((1, LANES), F32)],
        compiler_params=pltpu.CompilerParams(dimension_semantics=("arbitrary",), vmem_limit_bytes=VMEM_LIMIT),
        name="router",
    )(x, g.reshape(1, D), wr_hi, wr_lo)


def _dispatch_kernel(pos1_ref, pos2_ref, h3_ref, xs_in_ref, xs_ref, sem):
    del xs_in_ref
    tm = h3_ref.shape[0]
    base = pl.program_id(0) * tm

    def issue(blk, carry):
        for u in range(DMA_UNROLL):
            t = blk * DMA_UNROLL + u
            for k, pos_ref in enumerate((pos1_ref, pos2_ref)):
                pltpu.make_async_copy(h3_ref.at[t], xs_ref.at[pos_ref[base + t]], sem).start(priority=k)
        return carry

    lax.fori_loop(0, tm // DMA_UNROLL, issue, 0)
    for _ in range(2):
        pltpu.make_async_copy(h3_ref, xs_ref.at[pl.ds(0, tm)], sem).wait()


def _dispatch(pos1, pos2, h3, n_rows, *, tm):
    T = h3.shape[0]
    xs0 = jnp.zeros((n_rows, ROW_SUB, LANES), F32)
    return pl.pallas_call(
        _dispatch_kernel,
        grid_spec=pltpu.PrefetchScalarGridSpec(
            num_scalar_prefetch=2,
            grid=(T // tm,),
            in_specs=[pl.BlockSpec((tm, ROW_SUB, LANES), lambda i, *_: (i, 0, 0)),
                      pl.BlockSpec(memory_space=pl.ANY)],
            out_specs=pl.BlockSpec(memory_space=pl.ANY),
            scratch_shapes=[pltpu.SemaphoreType.DMA],
        ),
        out_shape=jax.ShapeDtypeStruct((n_rows, ROW_SUB, LANES), F32),
        input_output_aliases={3: 0},
        compiler_params=pltpu.CompilerParams(dimension_semantics=("arbitrary",)),
        name="dispatch",
    )(pos1, pos2, h3, xs0)


def _expert_kernel(te_ref, nu_ref, x3_ref, wg_ref, wu_ref, wd_ref, y3_ref, stage, h_scr):
    del te_ref
    j = pl.program_id(1)
    used = pl.program_id(0) < nu_ref[0]

    @pl.when(jnp.logical_not(used) & (j == 0))
    def _():
        y3_ref[...] = jnp.zeros_like(y3_ref)

    @pl.when(used)
    def _():
        @pl.when(j == 0)
        def _():
            _rows_to_stage(x3_ref, stage)
            h_scr[...] = _stage_load(stage).astype(BF16)

        h = h_scr[...]
        gate = jnp.dot(h, wg_ref[0], preferred_element_type=F32)
        up = jnp.dot(h, wu_ref[0], preferred_element_type=F32)
        act = (_silu(gate) * up).astype(BF16)
        part = jnp.dot(act, wd_ref[0], preferred_element_type=F32)

        @pl.when(j == 0)
        def _():
            _stage_store(stage, part)

        @pl.when(j > 0)
        def _():
            _stage_store(stage, part, accumulate=True)

        @pl.when(j == pl.num_programs(1) - 1)
        def _():
            _stage_to_rows(stage, y3_ref)


def _experts(tile_expert, n_used, xs3, w_gate, w_up, w_down, *, tmg, tf):
    n_rows = xs3.shape[0]
    E, D, F = w_gate.shape
    nj = F // tf
    row_blk = lambda i, j, te, nu: (jnp.minimum(i, nu[0] - 1), 0, 0)

    def chunk(i, j, nu):
        serp = lambda ii, jj: jnp.where(ii % 2 == 0, jj, nj - 1 - jj)
        return jnp.where(i < nu[0], serp(i, j), serp(nu[0] - 1, nj - 1))

    return pl.pallas_call(
        _expert_kernel,
        grid_spec=pltpu.PrefetchScalarGridSpec(
            num_scalar_prefetch=2,
            grid=(n_rows // tmg, nj),
            in_specs=[
                pl.BlockSpec((tmg, ROW_SUB, LANES), row_blk),
                pl.BlockSpec((1, D, tf), lambda i, j, te, nu: (te[i], 0, chunk(i, j, nu))),
                pl.BlockSpec((1, D, tf), lambda i, j, te, nu: (te[i], 0, chunk(i, j, nu))),
                pl.BlockSpec((1, tf, D), lambda i, j, te, nu: (te[i], chunk(i, j, nu), 0)),
            ],
            out_specs=pl.BlockSpec((tmg, ROW_SUB, LANES), lambda i, j, te, nu: (i, 0, 0)),
            scratch_shapes=[pltpu.VMEM((tmg // SUBLANES, ROW_SUB * SUBLANES, LANES), F32),
                            pltpu.VMEM((tmg, D), BF16)],
        ),
        out_shape=jax.ShapeDtypeStruct((n_rows, ROW_SUB, LANES), F32),
        compiler_params=pltpu.CompilerParams(
            dimension_semantics=("arbitrary", "arbitrary"), vmem_limit_bytes=VMEM_LIMIT),
        name="experts",
    )(tile_expert, n_used, xs3, w_gate.astype(BF16), w_up.astype(BF16), w_down.astype(BF16))


def _combine_kernel(pos1_ref, pos2_ref, x_ref, wts_ref, fg_ref, y3_ref, o_ref, rows, stage, sems):
    tm = x_ref.shape[0]
    i = pl.program_id(0)

    def issue(tile, slot):
        base = tile * tm

        def body(blk, carry):
            for u in range(DMA_UNROLL):
                t = blk * DMA_UNROLL + u
                for k, pos_ref in enumerate((pos1_ref, pos2_ref)):
                    pltpu.make_async_copy(y3_ref.at[pos_ref[base + t]], rows.at[slot, k, t],
                                          sems.at[slot]).start(priority=k)
            return carry

        lax.fori_loop(0, tm // DMA_UNROLL, body, 0)

    @pl.when(i == 0)
    def _():
        issue(0, 0)

    @pl.when(i + 1 < pl.num_programs(0))
    def _():
        issue(i + 1, (i + 1) % 2)

    slot = i % 2
    for k in range(2):
        pltpu.make_async_copy(y3_ref.at[pl.ds(0, tm)], rows.at[slot, k], sems.at[slot]).wait()

    w = wts_ref[...]
    _rows_to_stage(rows.at[slot, 0], stage)
    y1 = _stage_load(stage) * w[:, 0:1]
    _rows_to_stage(rows.at[slot, 1], stage)
    y2 = _stage_load(stage) * w[:, 1:2]
    o_ref[...] = _rms(x_ref[...] + (y1 + y2), fg_ref[...])


def _combine(pos1, pos2, x, wts, final_g, y3, *, tm):
    T, D = x.shape
    return pl.pallas_call(
        _combine_kernel,
        grid_spec=pltpu.PrefetchScalarGridSpec(
            num_scalar_prefetch=2,
            grid=(T // tm,),
            in_specs=[
                pl.BlockSpec((tm, D), lambda i, *_: (i, 0)),
                pl.BlockSpec((tm, LANES), lambda i, *_: (i, 0)),
                pl.BlockSpec((1, D), lambda i, *_: (0, 0)),
                pl.BlockSpec(memory_space=pl.ANY),
            ],
            out_specs=pl.BlockSpec((tm, D), lambda i, *_: (i, 0)),
            scratch_shapes=[pltpu.VMEM((2, 2, tm, ROW_SUB, LANES), F32),
                            pltpu.VMEM((tm // SUBLANES, ROW_SUB * SUBLANES, LANES), F32),
                            pltpu.SemaphoreType.DMA((2,))],
        ),
        out_shape=jax.ShapeDtypeStruct((T, D), F32),
        compiler_params=pltpu.CompilerParams(dimension_semantics=("arbitrary",), vmem_limit_bytes=VMEM_LIMIT),
        name="combine",
    )(pos1, pos2, x, wts, final_g.reshape(1, D), y3)


def _moe(x, g, w_router, w_gate, w_up, w_down, final_g, *, tm, tmg, tf):
    T, D = x.shape
    E = N_EXPERTS
    h3, keys, wts, cnt = _router(x, g, w_router, tm=tm)
    key1, key2 = keys[:, 0], keys[:, 1]

    counts = cnt[0, :E]
    tiles = (counts + (tmg - 1)) // tmg
    tile_end = jnp.cumsum(tiles)
    starts = ((tile_end - tiles) * tmg).astype(I32)
    n_tiles = 2 * T // tmg + E
    n_used = tile_end[-1:].astype(I32)
    last_expert = jnp.max(jnp.where(tiles > 0, jnp.arange(E), 0))
    tile_expert = jnp.sum(jnp.arange(n_tiles)[:, None] >= tile_end[None, :], axis=1)
    tile_expert = jnp.minimum(tile_expert, last_expert).astype(I32)

    pos1 = starts[key1 >> KEY_SHIFT] + (key1 & KEY_MASK)
    pos2 = starts[key2 >> KEY_SHIFT] + (key2 & KEY_MASK)

    xs3 = _dispatch(pos1, pos2, h3, n_tiles * tmg, tm=tm)
    y3 = _experts(tile_expert, n_used, xs3, w_gate, w_up, w_down, tmg=tmg, tf=tf)
    return _combine(pos1, pos2, x, wts, final_g, y3, tm=tm)


def kernel(x, attn_norm, ffn_norm, w_in, b_in, conv_w, conv_b, conv_ln_g, conv_ln_b, sinks, w_out, b_out,
           ffn_w_gate, ffn_w_up, ffn_w_down, moe_router, moe_w_gate, moe_w_up, moe_w_down, final_norm):
    B, S, D = x.shape
    ts = min(S, 512)
    tm = min(B * S, 512)

    def mixer(x, l):
        return _mixer(x, attn_norm[l], w_in[l], b_in[l], conv_w[l], conv_b[l], conv_ln_g[l], conv_ln_b[l],
                      sinks[l], w_out[l], b_out[l], ts=ts)

    x = mixer(x, 0)
    x = _ffn(x.reshape(B * S, D), ffn_norm[0], ffn_w_gate[0], ffn_w_up[0], ffn_w_down[0],
             tm=tm)
    x = mixer(x.reshape(B, S, D), 1)
    x = _moe(x.reshape(B * S, D), ffn_norm[1], moe_router[0], moe_w_gate[0], moe_w_up[0], moe_w_down[0],
             final_norm, tm=tm, tmg=tm, tf=moe_w_gate.shape[3] // 2)
    return x.reshape(B, S, D)
```

```python
import functools

import jax
import jax.numpy as jnp
from jax import lax
from jax.experimental import pallas as pl
from jax.experimental.pallas import tpu as pltpu

F32 = jnp.float32
BF16 = jnp.bfloat16
I32 = jnp.int32

D_MODEL = 1024
CONV_CH = 512
CONV_KERNEL = 31
HEAD_DIM = 64
N_Q_HEADS = 8
N_KV_HEADS = 2
ATTN_WIDTH = N_Q_HEADS * HEAD_DIM
KV_WIDTH = N_KV_HEADS * HEAD_DIM
BLOCK = 128
N_EXPERTS = 8
EPS = 1e-5

LANES = 128
SUBLANES = 8
CONV_HALO = 32
CONV_ROWS = 32
VMEM_LIMIT = 56 * 1024 * 1024

COL_Q = 2 * CONV_CH
COL_K = COL_Q + ATTN_WIDTH
COL_V = COL_K + KV_WIDTH
IN_COLS = COL_V + KV_WIDTH

ROW_SUB = D_MODEL // LANES
KEY_SHIFT = 20
KEY_MASK = (1 << KEY_SHIFT) - 1
DMA_UNROLL = 8


def _rms(x, g):
    ms = jnp.mean(x * x, axis=-1, keepdims=True)
    return x * lax.rsqrt(ms + EPS) * g


def _silu(x):
    return x * jax.nn.sigmoid(x)


def _mixer_kernel(x_ref, g_ref, win_ref, bin_ref, cw_ref, cb_ref, lng_ref, lnb_ref,
                  sink_ref, wout_ref, bout_ref, o_ref, halo, sbuf, kbuf, vbuf, ybuf, *, ts):
    first = pl.program_id(1) == 0
    n_chunks = CONV_CH // LANES
    n_blocks = ts // BLOCK

    def store_u(row0, val):
        for c in range(n_chunks):
            sbuf[c, pl.ds(2 * row0, val.shape[0], stride=2), :] = val[:, c * LANES:(c + 1) * LANES]

    store_u(0, jnp.where(first, 0.0, halo[...]))
    tap0 = CONV_HALO - (CONV_KERNEL - 1)

    def conv_rows(r0):
        accs = [jnp.broadcast_to(cb_ref[:, c * LANES:(c + 1) * LANES], (CONV_ROWS, LANES)) for c in range(n_chunks)]
        for j in range(CONV_KERNEL):
            for c in range(n_chunks):
                accs[c] = accs[c] + (cw_ref[j:j + 1, c * LANES:(c + 1) * LANES]
                                     * sbuf[c, pl.ds(2 * (r0 + tap0 + j), CONV_ROWS, stride=2), :])
        acc = jnp.concatenate(accs, axis=1)
        mu = jnp.mean(acc, axis=-1, keepdims=True)
        xc = acc - mu
        var = jnp.mean(xc * xc, axis=-1, keepdims=True)
        yn = xc * lax.rsqrt(var + EPS) * lng_ref[...] + lnb_ref[...]
        ybuf[r0:r0 + CONV_ROWS, 0:CONV_CH] = _silu(yn).astype(BF16)

    for buf in (kbuf, vbuf):
        buf[0:BLOCK, :] = jnp.where(first, jnp.zeros((BLOCK, 4 * LANES), BF16), buf[ts:ts + BLOCK, :])

    low_o = lax.broadcasted_iota(I32, (BLOCK, LANES), 1) < HEAD_DIM

    def store_kv(n, kv):
        rows = slice((n + 1) * BLOCK, (n + 2) * BLOCK)
        for src, buf in ((kv[:, 0:LANES], kbuf), (kv[:, LANES:2 * LANES], vbuf)):
            swapped = pltpu.roll(src, HEAD_DIM, axis=1)
            zero = jnp.zeros_like(src)
            buf[rows, 0 * LANES:1 * LANES] = jnp.where(low_o, src, zero).astype(BF16)
            buf[rows, 1 * LANES:2 * LANES] = jnp.where(low_o, zero, swapped).astype(BF16)
            buf[rows, 2 * LANES:3 * LANES] = jnp.where(low_o, swapped, zero).astype(BF16)
            buf[rows, 3 * LANES:4 * LANES] = jnp.where(low_o, zero, src).astype(BF16)

    qi = lax.broadcasted_iota(I32, (BLOCK, 2 * BLOCK), 0)
    kj = lax.broadcasted_iota(I32, (BLOCK, 2 * BLOCK), 1)
    band = (kj > qi) & (kj <= qi + BLOCK)
    band_first = band & ((kj >= BLOCK) | jnp.logical_not(first))

    def attend(n, q):
        rows = slice(n * BLOCK, (n + 2) * BLOCK)
        mask = band_first if n == 0 else band
        for hkv in range(N_KV_HEADS):
            k_bd = jnp.concatenate([kbuf[rows, (2 * hkv) * LANES:(2 * hkv + 1) * LANES],
                                    kbuf[rows, (2 * hkv + 1) * LANES:(2 * hkv + 2) * LANES]], axis=0)
            v_bd = jnp.concatenate([vbuf[rows, (2 * hkv) * LANES:(2 * hkv + 1) * LANES],
                                    vbuf[rows, (2 * hkv + 1) * LANES:(2 * hkv + 2) * LANES]], axis=0)
            for pair in range(2):
                hp = 2 * hkv + pair
                qp = q[:, hp * LANES:(hp + 1) * LANES]
                s = lax.dot_general(qp, k_bd, (((1,), (1,)), ((), ())),
                                    preferred_element_type=F32)
                ps, rden = [], []
                for hh in range(2):
                    sink = sink_ref[2 * hp + hh]
                    sh = jnp.where(mask, s[:, hh * 2 * BLOCK:(hh + 1) * 2 * BLOCK], -jnp.inf)
                    m = jnp.maximum(jnp.max(sh, axis=-1, keepdims=True), sink)
                    p = jnp.exp(sh - m)
                    den = jnp.sum(p, axis=-1, keepdims=True) + jnp.exp(sink - m)
                    ps.append(p.astype(BF16))
                    rden.append(1.0 / den)
                o = jnp.dot(jnp.concatenate(ps, axis=1), v_bd, preferred_element_type=F32)
                o = o * jnp.where(low_o, rden[0], rden[1])
                ybuf[n * BLOCK:(n + 1) * BLOCK, CONV_CH + hp * LANES:CONV_CH + (hp + 1) * LANES] = o.astype(BF16)

    x = x_ref[0]
    h = _rms(x, g_ref[...]).astype(BF16)
    ag = jnp.dot(h, win_ref[:, 0:COL_Q], preferred_element_type=F32) + bin_ref[:, 0:COL_Q]
    u = ag[:, :CONV_CH] * jax.nn.sigmoid(ag[:, CONV_CH:])
    store_u(CONV_HALO, u)
    halo[...] = u[ts - CONV_HALO:, :]
    q = jnp.dot(h, win_ref[:, COL_Q:COL_K], preferred_element_type=F32) + bin_ref[:, COL_Q:COL_K]
    q = (q * (HEAD_DIM ** -0.5)).astype(BF16)
    kv = jnp.dot(h, win_ref[:, COL_K:IN_COLS], preferred_element_type=F32) + bin_ref[:, COL_K:IN_COLS]

    for n in range(n_blocks):
        blk = slice(n * BLOCK, (n + 1) * BLOCK)
        store_kv(n, kv[blk, :])
        attend(n, q[blk, :])
        for r0 in range(n * BLOCK, (n + 1) * BLOCK, CONV_ROWS):
            conv_rows(r0)
        y = jnp.dot(ybuf[blk, :], wout_ref[...], preferred_element_type=F32)
        o_ref[0, blk, :] = x[blk, :] + y + bout_ref[...]


def _mixer(x, g, w_in, b_in, conv_w, conv_b, ln_g, ln_b, sinks, w_out, b_out, *, ts):
    B, S, D = x.shape
    row = lambda a: a.reshape(1, -1)
    const = lambda shape: pl.BlockSpec(shape, lambda b, s: (0,) * len(shape))
    return pl.pallas_call(
        functools.partial(_mixer_kernel, ts=ts),
        grid=(B, S // ts),
        in_specs=[
            pl.BlockSpec((1, ts, D), lambda b, s: (b, s, 0)),
            const((1, D)),
            const((D, IN_COLS)),
            const((1, IN_COLS)),
            const((CONV_KERNEL, CONV_CH)),
            const((1, CONV_CH)),
            const((1, CONV_CH)),
            const((1, CONV_CH)),
            pl.BlockSpec(memory_space=pltpu.SMEM),
            const((D, D)),
            const((1, D)),
        ],
        out_specs=pl.BlockSpec((1, ts, D), lambda b, s: (b, s, 0)),
        out_shape=jax.ShapeDtypeStruct((B, S, D), F32),
        scratch_shapes=[
            pltpu.VMEM((CONV_HALO, CONV_CH), F32),
            pltpu.VMEM((CONV_CH // LANES, 2 * (CONV_HALO + ts), LANES), F32),
            pltpu.VMEM((BLOCK + ts, 4 * LANES), BF16),
            pltpu.VMEM((BLOCK + ts, 4 * LANES), BF16),
            pltpu.VMEM((ts, D), BF16),
        ],
        compiler_params=pltpu.CompilerParams(
            dimension_semantics=("arbitrary", "arbitrary"), vmem_limit_bytes=VMEM_LIMIT),
        name="mixer",
    )(x, row(g), w_in.astype(BF16), row(b_in), conv_w, row(conv_b), row(ln_g), row(ln_b),
      sinks, w_out.astype(BF16), row(b_out))


def _ffn_kernel(x_ref, g_ref, wg_ref, wu_ref, wd_ref, o_ref):
    x = x_ref[...]
    h = _rms(x, g_ref[...]).astype(BF16)
    gate = jnp.dot(h, wg_ref[...], preferred_element_type=F32)
    up = jnp.dot(h, wu_ref[...], preferred_element_type=F32)
    a = (_silu(gate) * up).astype(BF16)
    o_ref[...] = x + jnp.dot(a, wd_ref[...], preferred_element_type=F32)


def _ffn(x, g, w_gate, w_up, w_down, *, tm):
    T, D = x.shape
    F = w_gate.shape[1]
    resident = lambda shape: pl.BlockSpec(shape, lambda i: (0, 0), pipeline_mode=pl.Buffered(1))
    return pl.pallas_call(
        _ffn_kernel,
        grid=(T // tm,),
        in_specs=[
            pl.BlockSpec((tm, D), lambda i: (i, 0)),
            pl.BlockSpec((1, D), lambda i: (0, 0)),
            resident((D, F)),
            resident((D, F)),
            resident((F, D)),
        ],
        out_specs=pl.BlockSpec((tm, D), lambda i: (i, 0)),
        out_shape=jax.ShapeDtypeStruct((T, D), F32),
        compiler_params=pltpu.CompilerParams(dimension_semantics=("arbitrary",), vmem_limit_bytes=VMEM_LIMIT),
        name="dense_ffn",
    )(x, g.reshape(1, D), w_gate.astype(BF16), w_up.astype(BF16), w_down.astype(BF16))


def _stage_store(stage_ref, val, accumulate=False):
    n = val.shape[0]
    for c in range(ROW_SUB):
        tile = val[:, c * LANES:(c + 1) * LANES].reshape(n // SUBLANES, SUBLANES, LANES)
        if accumulate:
            stage_ref[:, c * SUBLANES:(c + 1) * SUBLANES, :] += tile
        else:
            stage_ref[:, c * SUBLANES:(c + 1) * SUBLANES, :] = tile


def _stage_load(stage_ref):
    n = stage_ref.shape[0] * SUBLANES
    return jnp.concatenate(
        [stage_ref[:, c * SUBLANES:(c + 1) * SUBLANES, :].reshape(n, LANES) for c in range(ROW_SUB)], axis=1)


def _stage_to_rows(stage_ref, rows_ref):
    for g in range(stage_ref.shape[0]):
        for t in range(SUBLANES):
            rows_ref[g * SUBLANES + t] = stage_ref[g, pl.ds(t, ROW_SUB, stride=SUBLANES), :]


def _rows_to_stage(rows_ref, stage_ref):
    for g in range(stage_ref.shape[0]):
        for t in range(SUBLANES):
            stage_ref[g, pl.ds(t, ROW_SUB, stride=SUBLANES), :] = rows_ref[g * SUBLANES + t]


def _split_bf16(a):
    hi = a.astype(BF16)
    lo = (a - hi.astype(F32)).astype(BF16)
    return hi, lo


def _router_kernel(x_ref, g_ref, wr_hi_ref, wr_lo_ref, h3_ref, keys_ref, wts_ref, cnt_ref, stage, count):
    tm = x_ref.shape[0]
    lane = lax.broadcasted_iota(I32, (tm, LANES), 1)

    @pl.when(pl.program_id(0) == 0)
    def _():
        count[...] = jnp.zeros_like(count)

    hf = _rms(x_ref[...], g_ref[...])
    h_hi, h_lo = _split_bf16(hf)
    logits = (jnp.dot(h_hi, wr_hi_ref[...], preferred_element_type=F32)
              + jnp.dot(h_lo, wr_hi_ref[...], preferred_element_type=F32)
              + jnp.dot(h_hi, wr_lo_ref[...], preferred_element_type=F32))
    lg = jnp.where(lane < N_EXPERTS, logits, -jnp.inf)
    v1 = jnp.max(lg, axis=-1, keepdims=True)
    i1 = jnp.min(jnp.where(lg == v1, lane, LANES), axis=-1, keepdims=True)
    lg2 = jnp.where(lane == i1, -jnp.inf, lg)
    v2 = jnp.max(lg2, axis=-1, keepdims=True)
    i2 = jnp.min(jnp.where(lg2 == v2, lane, LANES), axis=-1, keepdims=True)
    t = jnp.exp(v2 - v1)
    w1 = 1.0 / (1.0 + t)
    w2 = t / (1.0 + t)

    onehot = jnp.where((lane == i1) | (lane == i2), 1.0, 0.0)
    r = lax.broadcasted_iota(I32, (tm, tm), 0)
    c = lax.broadcasted_iota(I32, (tm, tm), 1)
    earlier = jnp.where(c < r, 1.0, 0.0).astype(BF16)
    before = count[...] + jnp.dot(earlier, onehot.astype(BF16), preferred_element_type=F32)
    rank1 = jnp.sum(jnp.where(lane == i1, before, 0.0), axis=-1, keepdims=True).astype(I32)
    rank2 = jnp.sum(jnp.where(lane == i2, before, 0.0), axis=-1, keepdims=True).astype(I32)
    count[...] += jnp.sum(onehot, axis=0, keepdims=True)

    key1 = i1 * (1 << KEY_SHIFT) + rank1
    key2 = i2 * (1 << KEY_SHIFT) + rank2
    keys_ref[...] = jnp.where(lane == 0, key1, jnp.where(lane == 1, key2, 0))
    wts_ref[...] = jnp.where(lane == 0, w1, jnp.where(lane == 1, w2, 0.0))
    cnt_ref[...] = count[...].astype(I32)
    _stage_store(stage, hf)
    _stage_to_rows(stage, h3_ref)


def _router(x, g, w_router, *, tm):
    T, D = x.shape
    wr = jnp.pad(w_router, ((0, 0), (0, LANES - N_EXPERTS)))
    wr_hi = wr.astype(BF16)
    wr_lo = (wr - wr_hi.astype(F32)).astype(BF16)
    return pl.pallas_call(
        _router_kernel,
        grid=(T // tm,),
        in_specs=[
            pl.BlockSpec((tm, D), lambda i: (i, 0)),
            pl.BlockSpec((1, D), lambda i: (0, 0)),
            pl.BlockSpec((D, LANES), lambda i: (0, 0)),
            pl.BlockSpec((D, LANES), lambda i: (0, 0)),
        ],
        out_specs=[
            pl.BlockSpec((tm, ROW_SUB, LANES), lambda i: (i, 0, 0)),
            pl.BlockSpec((tm, LANES), lambda i: (i, 0)),
            pl.BlockSpec((tm, LANES), lambda i: (i, 0)),
            pl.BlockSpec((1, LANES), lambda i: (0, 0)),
        ],
        out_shape=[
            jax.ShapeDtypeStruct((T, ROW_SUB, LANES), F32),
            jax.ShapeDtypeStruct((T, LANES), I32),
            jax.ShapeDtypeStruct((T, LANES), F32),
            jax.ShapeDtypeStruct((1, LANES), I32),
        ],
        scratch_shapes=[pltpu.VMEM((tm // SUBLANES, ROW_SUB * SUBLANES, LANES), F32),
                        pltpu.VMEM((1, LANES), F32)],
        compiler_params=pltpu.CompilerParams(dimension_semantics=("arbitrary",), vmem_limit_bytes=VMEM_LIMIT),
        name="router",
    )(x, g.reshape(1, D), wr_hi, wr_lo)


def _dispatch_kernel(pos1_ref, pos2_ref, pad_start_ref, pad_len_ref, nu_ref, h3_ref, xs_ref,
                     zeros, sem, zsem, *, tmg):
    tm = h3_ref.shape[0]
    base = pl.program_id(0) * tm

    @pl.when(pl.program_id(0) == 0)
    def _():
        zeros[...] = jnp.zeros_like(zeros)

        def pad_copies(act):
            for e in range(N_EXPERTS):
                start, length = pad_start_ref[e], pad_len_ref[e]
                for b in range(tmg.bit_length() - 1):
                    size = 1 << b
                    offset = start + ((length >> (b + 1)) << (b + 1))
                    copy = pltpu.make_async_copy(zeros.at[pl.ds(0, size)], xs_ref.at[pl.ds(offset, size)], zsem)
                    pl.when(((length >> b) & 1) == 1)(lambda copy=copy: act(copy))

        def tile_copies(act):
            def body(i, carry):
                act(pltpu.make_async_copy(zeros, xs_ref.at[pl.ds(i * tmg, tmg)], zsem))
                return carry

            lax.fori_loop(nu_ref[0], xs_ref.shape[0] // tmg, body, 0)

        for act in (lambda c: c.start(), lambda c: c.wait()):
            pad_copies(act)
            tile_copies(act)

    def issue(blk, carry):
        for u in range(DMA_UNROLL):
            t = blk * DMA_UNROLL + u
            for k, pos_ref in enumerate((pos1_ref, pos2_ref)):
                pltpu.make_async_copy(h3_ref.at[t], xs_ref.at[pos_ref[base + t]], sem).start(priority=k)
        return carry

    lax.fori_loop(0, tm // DMA_UNROLL, issue, 0)
    for _ in range(2):
        pltpu.make_async_copy(h3_ref, xs_ref.at[pl.ds(0, tm)], sem).wait()


def _dispatch(pos1, pos2, pad_start, pad_len, n_used, h3, n_rows, *, tm, tmg):
    T = h3.shape[0]
    return pl.pallas_call(
        functools.partial(_dispatch_kernel, tmg=tmg),
        grid_spec=pltpu.PrefetchScalarGridSpec(
            num_scalar_prefetch=5,
            grid=(T // tm,),
            in_specs=[pl.BlockSpec((tm, ROW_SUB, LANES), lambda i, *_: (i, 0, 0))],
            out_specs=pl.BlockSpec(memory_space=pl.ANY),
            scratch_shapes=[pltpu.VMEM((tmg, ROW_SUB, LANES), F32),
                            pltpu.SemaphoreType.DMA, pltpu.SemaphoreType.DMA],
        ),
        out_shape=jax.ShapeDtypeStruct((n_rows, ROW_SUB, LANES), F32),
        compiler_params=pltpu.CompilerParams(dimension_semantics=("arbitrary",)),
        name="dispatch",
    )(pos1, pos2, pad_start, pad_len, n_used, h3)


def _expert_kernel(te_ref, nu_ref, x3_ref, wg_ref, wu_ref, wd_ref, y3_ref, stage, h_scr):
    del te_ref
    j = pl.program_id(1)
    used = pl.program_id(0) < nu_ref[0]

    @pl.when(jnp.logical_not(used) & (j == 0))
    def _():
        y3_ref[...] = jnp.zeros_like(y3_ref)

    @pl.when(used)
    def _():
        @pl.when(j == 0)
        def _():
            _rows_to_stage(x3_ref, stage)
            h_scr[...] = _stage_load(stage).astype(BF16)

        h = h_scr[...]
        gate = jnp.dot(h, wg_ref[0], preferred_element_type=F32)
        up = jnp.dot(h, wu_ref[0], preferred_element_type=F32)
        act = (_silu(gate) * up).astype(BF16)
        part = jnp.dot(act, wd_ref[0], preferred_element_type=F32)

        @pl.when(j == 0)
        def _():
            _stage_store(stage, part)

        @pl.when(j > 0)
        def _():
            _stage_store(stage, part, accumulate=True)

        @pl.when(j == pl.num_programs(1) - 1)
        def _():
            _stage_to_rows(stage, y3_ref)


def _experts(tile_expert, n_used, xs3, w_gate, w_up, w_down, *, tmg, tf):
    n_rows = xs3.shape[0]
    E, D, F = w_gate.shape
    nj = F // tf
    row_blk = lambda i, j, te, nu: (jnp.minimum(i, nu[0] - 1), 0, 0)

    def chunk(i, j, nu):
        serp = lambda ii, jj: jnp.where(ii % 2 == 0, jj, nj - 1 - jj)
        return jnp.where(i < nu[0], serp(i, j), serp(nu[0] - 1, nj - 1))

    return pl.pallas_call(
        _expert_kernel,
        grid_spec=pltpu.PrefetchScalarGridSpec(
            num_scalar_prefetch=2,
            grid=(n_rows // tmg, nj),
            in_specs=[
                pl.BlockSpec((tmg, ROW_SUB, LANES), row_blk),
                pl.BlockSpec((1, D, tf), lambda i, j, te, nu: (te[i], 0, chunk(i, j, nu))),
                pl.BlockSpec((1, D, tf), lambda i, j, te, nu: (te[i], 0, chunk(i, j, nu))),
                pl.BlockSpec((1, tf, D), lambda i, j, te, nu: (te[i], chunk(i, j, nu), 0)),
            ],
            out_specs=pl.BlockSpec((tmg, ROW_SUB, LANES), lambda i, j, te, nu: (i, 0, 0)),
            scratch_shapes=[pltpu.VMEM((tmg // SUBLANES, ROW_SUB * SUBLANES, LANES), F32),
                            pltpu.VMEM((tmg, D), BF16)],
        ),
        out_shape=jax.ShapeDtypeStruct((n_rows, ROW_SUB, LANES), F32),
        compiler_params=pltpu.CompilerParams(
            dimension_semantics=("arbitrary", "arbitrary"), vmem_limit_bytes=VMEM_LIMIT),
        name="experts",
    )(tile_expert, n_used, xs3, w_gate.astype(BF16), w_up.astype(BF16), w_down.astype(BF16))


def _combine_kernel(pos1_ref, pos2_ref, x_ref, wts_ref, fg_ref, y3_ref, o_ref, rows, stage, sems):
    tm = x_ref.shape[0]
    i = pl.program_id(0)

    def issue(tile, slot):
        base = tile * tm

        def body(blk, carry):
            for u in range(DMA_UNROLL):
                t = blk * DMA_UNROLL + u
                for k, pos_ref in enumerate((pos1_ref, pos2_ref)):
                    pltpu.make_async_copy(y3_ref.at[pos_ref[base + t]], rows.at[slot, k, t],
                                          sems.at[slot]).start(priority=k)
            return carry

        lax.fori_loop(0, tm // DMA_UNROLL, body, 0)

    @pl.when(i == 0)
    def _():
        issue(0, 0)

    @pl.when(i + 1 < pl.num_programs(0))
    def _():
        issue(i + 1, (i + 1) % 2)

    slot = i % 2
    for k in range(2):
        pltpu.make_async_copy(y3_ref.at[pl.ds(0, tm)], rows.at[slot, k], sems.at[slot]).wait()

    w = wts_ref[...]
    _rows_to_stage(rows.at[slot, 0], stage)
    y1 = _stage_load(stage) * w[:, 0:1]
    _rows_to_stage(rows.at[slot, 1], stage)
    y2 = _stage_load(stage) * w[:, 1:2]
    o_ref[...] = _rms(x_ref[...] + (y1 + y2), fg_ref[...])


def _combine(pos1, pos2, x, wts, final_g, y3, *, tm):
    T, D = x.shape
    return pl.pallas_call(
        _combine_kernel,
        grid_spec=pltpu.PrefetchScalarGridSpec(
            num_scalar_prefetch=2,
            grid=(T // tm,),
            in_specs=[
                pl.BlockSpec((tm, D), lambda i, *_: (i, 0)),
                pl.BlockSpec((tm, LANES), lambda i, *_: (i, 0)),
                pl.BlockSpec((1, D), lambda i, *_: (0, 0)),
                pl.BlockSpec(memory_space=pl.ANY),
            ],
            out_specs=pl.BlockSpec((tm, D), lambda i, *_: (i, 0)),
            scratch_shapes=[pltpu.VMEM((2, 2, tm, ROW_SUB, LANES), F32),
                            pltpu.VMEM((tm // SUBLANES, ROW_SUB * SUBLANES, LANES), F32),
                            pltpu.SemaphoreType.DMA((2,))],
        ),
        out_shape=jax.ShapeDtypeStruct((T, D), F32),
        compiler_params=pltpu.CompilerParams(dimension_semantics=("arbitrary",), vmem_limit_bytes=VMEM_LIMIT),
        name="combine",
    )(pos1, pos2, x, wts, final_g.reshape(1, D), y3)


def _moe(x, g, w_router, w_gate, w_up, w_down, final_g, *, tm, tmg, tf):
    T, D = x.shape
    E = N_EXPERTS
    h3, keys, wts, cnt = _router(x, g, w_router, tm=tm)
    key1, key2 = keys[:, 0], keys[:, 1]

    counts = cnt[0, :E]
    tiles = (counts + (tmg - 1)) // tmg
    tile_end = jnp.cumsum(tiles)
    starts = ((tile_end - tiles) * tmg).astype(I32)
    n_tiles = 2 * T // tmg + E
    n_used = tile_end[-1:].astype(I32)
    last_expert = jnp.max(jnp.where(tiles > 0, jnp.arange(E), 0))
    tile_expert = jnp.sum(jnp.arange(n_tiles)[:, None] >= tile_end[None, :], axis=1)
    tile_expert = jnp.minimum(tile_expert, last_expert).astype(I32)

    pos1 = starts[key1 >> KEY_SHIFT] + (key1 & KEY_MASK)
    pos2 = starts[key2 >> KEY_SHIFT] + (key2 & KEY_MASK)

    pad_start = (starts + counts).astype(I32)
    pad_len = (tiles * tmg - counts).astype(I32)
    xs3 = _dispatch(pos1, pos2, pad_start, pad_len, n_used, h3, n_tiles * tmg, tm=tm, tmg=tmg)
    y3 = _experts(tile_expert, n_used, xs3, w_gate, w_up, w_down, tmg=tmg, tf=tf)
    return _combine(pos1, pos2, x, wts, final_g, y3, tm=tm)


def kernel(x, attn_norm, ffn_norm, w_in, b_in, conv_w, conv_b, conv_ln_g, conv_ln_b, sinks, w_out, b_out,
           ffn_w_gate, ffn_w_up, ffn_w_down, moe_router, moe_w_gate, moe_w_up, moe_w_down, final_norm):
    B, S, D = x.shape
    ts = min(S, 512)
    tm = min(B * S, 512)

    def mixer(x, l):
        return _mixer(x, attn_norm[l], w_in[l], b_in[l], conv_w[l], conv_b[l], conv_ln_g[l], conv_ln_b[l],
                      sinks[l], w_out[l], b_out[l], ts=ts)

    x = mixer(x, 0)
    x = _ffn(x.reshape(B * S, D), ffn_norm[0], ffn_w_gate[0], ffn_w_up[0], ffn_w_down[0],
             tm=tm)
    x = mixer(x.reshape(B, S, D), 1)
    x = _moe(x.reshape(B * S, D), ffn_norm[1], moe_router[0], moe_w_gate[0], moe_w_up[0], moe_w_down[0],
             final_norm, tm=tm, tmg=tm, tf=moe_w_gate.shape[3] // 2)
    return x.reshape(B, S, D)
```

```python
import functools

import jax
import jax.numpy as jnp
from jax import lax
from jax.experimental import pallas as pl
from jax.experimental.pallas import tpu as pltpu

F32 = jnp.float32
BF16 = jnp.bfloat16
I32 = jnp.int32

D_MODEL = 1024
CONV_CH = 512
CONV_KERNEL = 31
HEAD_DIM = 64
N_Q_HEADS = 8
N_KV_HEADS = 2
ATTN_WIDTH = N_Q_HEADS * HEAD_DIM
KV_WIDTH = N_KV_HEADS * HEAD_DIM
BLOCK = 128
N_EXPERTS = 8
EPS = 1e-5

LANES = 128
SUBLANES = 8
CONV_HALO = 32
CONV_ROWS = 32
VMEM_LIMIT = 56 * 1024 * 1024

COL_Q = 2 * CONV_CH
COL_K = COL_Q + ATTN_WIDTH
COL_V = COL_K + KV_WIDTH
IN_COLS = COL_V + KV_WIDTH

ROW_SUB = D_MODEL // LANES
KEY_SHIFT = 20
KEY_MASK = (1 << KEY_SHIFT) - 1
DMA_UNROLL = 8


def _rms(x, g):
    ms = jnp.mean(x * x, axis=-1, keepdims=True)
    return x * lax.rsqrt(ms + EPS) * g


def _silu(x):
    return x * jax.nn.sigmoid(x)


def _mixer_kernel(x_ref, g_ref, win_ref, bin_ref, cw_ref, cb_ref, lng_ref, lnb_ref,
                  sink_ref, wout_ref, bout_ref, o_ref, halo, sbuf, kbuf, vbuf, ybuf, *, ts):
    first = pl.program_id(1) == 0
    n_chunks = CONV_CH // LANES
    n_blocks = ts // BLOCK

    def store_u(row0, val):
        for c in range(n_chunks):
            sbuf[c, pl.ds(2 * row0, val.shape[0], stride=2), :] = val[:, c * LANES:(c + 1) * LANES]

    store_u(0, jnp.where(first, 0.0, halo[...]))
    tap0 = CONV_HALO - (CONV_KERNEL - 1)

    def conv_rows(r0):
        accs = [jnp.broadcast_to(cb_ref[:, c * LANES:(c + 1) * LANES], (CONV_ROWS, LANES)) for c in range(n_chunks)]
        for j in range(CONV_KERNEL):
            for c in range(n_chunks):
                accs[c] = accs[c] + (cw_ref[j:j + 1, c * LANES:(c + 1) * LANES]
                                     * sbuf[c, pl.ds(2 * (r0 + tap0 + j), CONV_ROWS, stride=2), :])
        acc = jnp.concatenate(accs, axis=1)
        mu = jnp.mean(acc, axis=-1, keepdims=True)
        xc = acc - mu
        var = jnp.mean(xc * xc, axis=-1, keepdims=True)
        yn = xc * lax.rsqrt(var + EPS) * lng_ref[...] + lnb_ref[...]
        ybuf[r0:r0 + CONV_ROWS, 0:CONV_CH] = _silu(yn).astype(BF16)

    for buf in (kbuf, vbuf):
        buf[0:BLOCK, :] = jnp.where(first, jnp.zeros((BLOCK, 4 * LANES), BF16), buf[ts:ts + BLOCK, :])

    low_o = lax.broadcasted_iota(I32, (BLOCK, LANES), 1) < HEAD_DIM

    def store_kv(n, kv):
        rows = slice((n + 1) * BLOCK, (n + 2) * BLOCK)
        for src, buf in ((kv[:, 0:LANES], kbuf), (kv[:, LANES:2 * LANES], vbuf)):
            swapped = pltpu.roll(src, HEAD_DIM, axis=1)
            zero = jnp.zeros_like(src)
            buf[rows, 0 * LANES:1 * LANES] = jnp.where(low_o, src, zero).astype(BF16)
            buf[rows, 1 * LANES:2 * LANES] = jnp.where(low_o, zero, swapped).astype(BF16)
            buf[rows, 2 * LANES:3 * LANES] = jnp.where(low_o, swapped, zero).astype(BF16)
            buf[rows, 3 * LANES:4 * LANES] = jnp.where(low_o, zero, src).astype(BF16)

    qi = lax.broadcasted_iota(I32, (BLOCK, 2 * BLOCK), 0)
    kj = lax.broadcasted_iota(I32, (BLOCK, 2 * BLOCK), 1)
    band = (kj > qi) & (kj <= qi + BLOCK)
    band_first = band & ((kj >= BLOCK) | jnp.logical_not(first))

    def attend(n, q):
        rows = slice(n * BLOCK, (n + 2) * BLOCK)
        mask = band_first if n == 0 else band
        for hkv in range(N_KV_HEADS):
            k_bd = jnp.concatenate([kbuf[rows, (2 * hkv) * LANES:(2 * hkv + 1) * LANES],
                                    kbuf[rows, (2 * hkv + 1) * LANES:(2 * hkv + 2) * LANES]], axis=0)
            v_bd = jnp.concatenate([vbuf[rows, (2 * hkv) * LANES:(2 * hkv + 1) * LANES],
                                    vbuf[rows, (2 * hkv + 1) * LANES:(2 * hkv + 2) * LANES]], axis=0)
            for pair in range(2):
                hp = 2 * hkv + pair
                qp = q[:, hp * LANES:(hp + 1) * LANES]
                s = lax.dot_general(qp, k_bd, (((1,), (1,)), ((), ())),
                                    preferred_element_type=F32)
                ps, rden = [], []
                for hh in range(2):
                    sink = sink_ref[2 * hp + hh]
                    sh = jnp.where(mask, s[:, hh * 2 * BLOCK:(hh + 1) * 2 * BLOCK], -jnp.inf)
                    m = jnp.maximum(jnp.max(sh, axis=-1, keepdims=True), sink)
                    p = jnp.exp(sh - m)
                    den = jnp.sum(p, axis=-1, keepdims=True) + jnp.exp(sink - m)
                    ps.append(p.astype(BF16))
                    rden.append(1.0 / den)
                o = jnp.dot(jnp.concatenate(ps, axis=1), v_bd, preferred_element_type=F32)
                o = o * jnp.where(low_o, rden[0], rden[1])
                ybuf[n * BLOCK:(n + 1) * BLOCK, CONV_CH + hp * LANES:CONV_CH + (hp + 1) * LANES] = o.astype(BF16)

    x = x_ref[0]
    h = _rms(x, g_ref[...]).astype(BF16)
    ag = jnp.dot(h, win_ref[:, 0:COL_Q], preferred_element_type=F32) + bin_ref[:, 0:COL_Q]
    u = ag[:, :CONV_CH] * jax.nn.sigmoid(ag[:, CONV_CH:])
    store_u(CONV_HALO, u)
    halo[...] = u[ts - CONV_HALO:, :]
    q = jnp.dot(h, win_ref[:, COL_Q:COL_K], preferred_element_type=F32) + bin_ref[:, COL_Q:COL_K]
    q = (q * (HEAD_DIM ** -0.5)).astype(BF16)
    kv = jnp.dot(h, win_ref[:, COL_K:IN_COLS], preferred_element_type=F32) + bin_ref[:, COL_K:IN_COLS]

    for n in range(n_blocks):
        blk = slice(n * BLOCK, (n + 1) * BLOCK)
        store_kv(n, kv[blk, :])
        attend(n, q[blk, :])
        for r0 in range(n * BLOCK, (n + 1) * BLOCK, CONV_ROWS):
            conv_rows(r0)
        y = jnp.dot(ybuf[blk, :], wout_ref[...], preferred_element_type=F32)
        o_ref[0, blk, :] = x[blk, :] + y + bout_ref[...]


def _mixer(x, g, w_in, b_in, conv_w, conv_b, ln_g, ln_b, sinks, w_out, b_out, *, ts):
    B, S, D = x.shape
    row = lambda a: a.reshape(1, -1)
    const = lambda shape: pl.BlockSpec(shape, lambda b, s: (0,) * len(shape))
    return pl.pallas_call(
        functools.partial(_mixer_kernel, ts=ts),
        grid=(B, S // ts),
        in_specs=[
            pl.BlockSpec((1, ts, D), lambda b, s: (b, s, 0)),
            const((1, D)),
            const((D, IN_COLS)),
            const((1, IN_COLS)),
            const((CONV_KERNEL, CONV_CH)),
            const((1, CONV_CH)),
            const((1, CONV_CH)),
            const((1, CONV_CH)),
            pl.BlockSpec(memory_space=pltpu.SMEM),
            const((D, D)),
            const((1, D)),
        ],
        out_specs=pl.BlockSpec((1, ts, D), lambda b, s: (b, s, 0)),
        out_shape=jax.ShapeDtypeStruct((B, S, D), F32),
        scratch_shapes=[
            pltpu.VMEM((CONV_HALO, CONV_CH), F32),
            pltpu.VMEM((CONV_CH // LANES, 2 * (CONV_HALO + ts), LANES), F32),
            pltpu.VMEM((BLOCK + ts, 4 * LANES), BF16),
            pltpu.VMEM((BLOCK + ts, 4 * LANES), BF16),
            pltpu.VMEM((ts, D), BF16),
        ],
        compiler_params=pltpu.CompilerParams(
            dimension_semantics=("arbitrary", "arbitrary"), vmem_limit_bytes=VMEM_LIMIT),
        name="mixer",
    )(x, row(g), w_in.astype(BF16), row(b_in), conv_w, row(conv_b), row(ln_g), row(ln_b),
      sinks, w_out.astype(BF16), row(b_out))


def _ffn_kernel(x_ref, g_ref, wg_ref, wu_ref, wd_ref, o_ref):
    x = x_ref[...]
    h = _rms(x, g_ref[...]).astype(BF16)
    gate = jnp.dot(h, wg_ref[...], preferred_element_type=F32)
    up = jnp.dot(h, wu_ref[...], preferred_element_type=F32)
    a = (_silu(gate) * up).astype(BF16)
    o_ref[...] = x + jnp.dot(a, wd_ref[...], preferred_element_type=F32)


def _ffn(x, g, w_gate, w_up, w_down, *, tm):
    T, D = x.shape
    F = w_gate.shape[1]
    resident = lambda shape: pl.BlockSpec(shape, lambda i: (0, 0), pipeline_mode=pl.Buffered(1))
    return pl.pallas_call(
        _ffn_kernel,
        grid=(T // tm,),
        in_specs=[
            pl.BlockSpec((tm, D), lambda i: (i, 0)),
            pl.BlockSpec((1, D), lambda i: (0, 0)),
            resident((D, F)),
            resident((D, F)),
            resident((F, D)),
        ],
        out_specs=pl.BlockSpec((tm, D), lambda i: (i, 0)),
        out_shape=jax.ShapeDtypeStruct((T, D), F32),
        compiler_params=pltpu.CompilerParams(dimension_semantics=("arbitrary",), vmem_limit_bytes=VMEM_LIMIT),
        name="dense_ffn",
    )(x, g.reshape(1, D), w_gate.astype(BF16), w_up.astype(BF16), w_down.astype(BF16))


def _stage_store(stage_ref, val, accumulate=False):
    n = val.shape[0]
    for c in range(ROW_SUB):
        tile = val[:, c * LANES:(c + 1) * LANES].reshape(n // SUBLANES, SUBLANES, LANES)
        if accumulate:
            stage_ref[:, c * SUBLANES:(c + 1) * SUBLANES, :] += tile
        else:
            stage_ref[:, c * SUBLANES:(c + 1) * SUBLANES, :] = tile


def _stage_load(stage_ref):
    n = stage_ref.shape[0] * SUBLANES
    return jnp.concatenate(
        [stage_ref[:, c * SUBLANES:(c + 1) * SUBLANES, :].reshape(n, LANES) for c in range(ROW_SUB)], axis=1)


def _stage_to_rows(stage_ref, rows_ref):
    for g in range(stage_ref.shape[0]):
        for t in range(SUBLANES):
            rows_ref[g * SUBLANES + t] = stage_ref[g, pl.ds(t, ROW_SUB, stride=SUBLANES), :]


def _rows_to_stage(rows_ref, stage_ref):
    for g in range(stage_ref.shape[0]):
        for t in range(SUBLANES):
            stage_ref[g, pl.ds(t, ROW_SUB, stride=SUBLANES), :] = rows_ref[g * SUBLANES + t]


def _split_bf16(a):
    hi = a.astype(BF16)
    lo = (a - hi.astype(F32)).astype(BF16)
    return hi, lo


def _router_kernel(x_ref, g_ref, wr_hi_ref, wr_lo_ref, h3_ref, keys_ref, wts_ref, cnt_ref, stage, count):
    tm = x_ref.shape[0]
    lane = lax.broadcasted_iota(I32, (tm, LANES), 1)

    @pl.when(pl.program_id(0) == 0)
    def _():
        count[...] = jnp.zeros_like(count)

    hf = _rms(x_ref[...], g_ref[...])
    h_hi, h_lo = _split_bf16(hf)
    logits = (jnp.dot(h_hi, wr_hi_ref[...], preferred_element_type=F32)
              + jnp.dot(h_lo, wr_hi_ref[...], preferred_element_type=F32)
              + jnp.dot(h_hi, wr_lo_ref[...], preferred_element_type=F32))
    lg = jnp.where(lane < N_EXPERTS, logits, -jnp.inf)
    v1 = jnp.max(lg, axis=-1, keepdims=True)
    i1 = jnp.min(jnp.where(lg == v1, lane, LANES), axis=-1, keepdims=True)
    lg2 = jnp.where(lane == i1, -jnp.inf, lg)
    v2 = jnp.max(lg2, axis=-1, keepdims=True)
    i2 = jnp.min(jnp.where(lg2 == v2, lane, LANES), axis=-1, keepdims=True)
    t = jnp.exp(v2 - v1)
    w1 = 1.0 / (1.0 + t)
    w2 = t / (1.0 + t)

    onehot = jnp.where((lane == i1) | (lane == i2), 1.0, 0.0)
    r = lax.broadcasted_iota(I32, (tm, tm), 0)
    c = lax.broadcasted_iota(I32, (tm, tm), 1)
    earlier = jnp.where(c < r, 1.0, 0.0).astype(BF16)
    before = count[...] + jnp.dot(earlier, onehot.astype(BF16), preferred_element_type=F32)
    rank1 = jnp.sum(jnp.where(lane == i1, before, 0.0), axis=-1, keepdims=True).astype(I32)
    rank2 = jnp.sum(jnp.where(lane == i2, before, 0.0), axis=-1, keepdims=True).astype(I32)
    count[...] += jnp.sum(onehot, axis=0, keepdims=True)

    key1 = i1 * (1 << KEY_SHIFT) + rank1
    key2 = i2 * (1 << KEY_SHIFT) + rank2
    keys_ref[...] = jnp.where(lane == 0, key1, jnp.where(lane == 1, key2, 0))
    wts_ref[...] = jnp.where(lane == 0, w1, jnp.where(lane == 1, w2, 0.0))
    cnt_ref[...] = count[...].astype(I32)
    _stage_store(stage, hf)
    _stage_to_rows(stage, h3_ref)


def _router(x, g, w_router, *, tm):
    T, D = x.shape
    wr = jnp.pad(w_router, ((0, 0), (0, LANES - N_EXPERTS)))
    wr_hi = wr.astype(BF16)
    wr_lo = (wr - wr_hi.astype(F32)).astype(BF16)
    return pl.pallas_call(
        _router_kernel,
        grid=(T // tm,),
        in_specs=[
            pl.BlockSpec((tm, D), lambda i: (i, 0)),
            pl.BlockSpec((1, D), lambda i: (0, 0)),
            pl.BlockSpec((D, LANES), lambda i: (0, 0)),
            pl.BlockSpec((D, LANES), lambda i: (0, 0)),
        ],
        out_specs=[
            pl.BlockSpec((tm, ROW_SUB, LANES), lambda i: (i, 0, 0)),
            pl.BlockSpec((tm, LANES), lambda i: (i, 0)),
            pl.BlockSpec((tm, LANES), lambda i: (i, 0)),
            pl.BlockSpec((1, LANES), lambda i: (0, 0)),
        ],
        out_shape=[
            jax.ShapeDtypeStruct((T, ROW_SUB, LANES), F32),
            jax.ShapeDtypeStruct((T, LANES), I32),
            jax.ShapeDtypeStruct((T, LANES), F32),
            jax.ShapeDtypeStruct((1, LANES), I32),
        ],
        scratch_shapes=[pltpu.VMEM((tm // SUBLANES, ROW_SUB * SUBLANES, LANES), F32),
                        pltpu.VMEM((1, LANES), F32)],
        compiler_params=pltpu.CompilerParams(dimension_semantics=("arbitrary",), vmem_limit_bytes=VMEM_LIMIT),
        name="router",
    )(x, g.reshape(1, D), wr_hi, wr_lo)


def _dispatch_kernel(pos1_ref, pos2_ref, pad_start_ref, pad_len_ref, nu_ref, h3_ref, xs_ref,
                     zeros, sem, zsem, *, tmg):
    tm = h3_ref.shape[0]
    base = pl.program_id(0) * tm

    @pl.when(pl.program_id(0) == 0)
    def _():
        zeros[...] = jnp.zeros_like(zeros)

        def pad_copies(act):
            for e in range(N_EXPERTS):
                start, length = pad_start_ref[e], pad_len_ref[e]
                for b in range(tmg.bit_length() - 1):
                    size = 1 << b
                    offset = start + ((length >> (b + 1)) << (b + 1))
                    copy = pltpu.make_async_copy(zeros.at[pl.ds(0, size)], xs_ref.at[pl.ds(offset, size)], zsem)
                    pl.when(((length >> b) & 1) == 1)(lambda copy=copy: act(copy))

        def tile_copies(act):
            def body(i, carry):
                act(pltpu.make_async_copy(zeros, xs_ref.at[pl.ds(i * tmg, tmg)], zsem))
                return carry

            lax.fori_loop(nu_ref[0], xs_ref.shape[0] // tmg, body, 0)

        for act in (lambda c: c.start(), lambda c: c.wait()):
            pad_copies(act)
            tile_copies(act)

    def issue(blk, carry):
        for u in range(DMA_UNROLL):
            t = blk * DMA_UNROLL + u
            for k, pos_ref in enumerate((pos1_ref, pos2_ref)):
                pltpu.make_async_copy(h3_ref.at[t], xs_ref.at[pos_ref[base + t]], sem).start(priority=k)
        return carry

    lax.fori_loop(0, tm // DMA_UNROLL, issue, 0)
    for _ in range(2):
        pltpu.make_async_copy(h3_ref, xs_ref.at[pl.ds(0, tm)], sem).wait()


def _dispatch(pos1, pos2, pad_start, pad_len, n_used, h3, n_rows, *, tm, tmg):
    T = h3.shape[0]
    return pl.pallas_call(
        functools.partial(_dispatch_kernel, tmg=tmg),
        grid_spec=pltpu.PrefetchScalarGridSpec(
            num_scalar_prefetch=5,
            grid=(T // tm,),
            in_specs=[pl.BlockSpec((tm, ROW_SUB, LANES), lambda i, *_: (i, 0, 0))],
            out_specs=pl.BlockSpec(memory_space=pl.ANY),
            scratch_shapes=[pltpu.VMEM((tmg, ROW_SUB, LANES), F32),
                            pltpu.SemaphoreType.DMA, pltpu.SemaphoreType.DMA],
        ),
        out_shape=jax.ShapeDtypeStruct((n_rows, ROW_SUB, LANES), F32),
        compiler_params=pltpu.CompilerParams(dimension_semantics=("arbitrary",)),
        name="dispatch",
    )(pos1, pos2, pad_start, pad_len, n_used, h3)


def _expert_kernel(te_ref, nu_ref, x3_ref, wg_ref, wu_ref, wd_ref, y3_ref, stage, h_scr):
    del te_ref
    j = pl.program_id(1)
    used = pl.program_id(0) < nu_ref[0]

    @pl.when(jnp.logical_not(used) & (j == 0))
    def _():
        y3_ref[...] = jnp.zeros_like(y3_ref)

    @pl.when(used)
    def _():
        @pl.when(j == 0)
        def _():
            _rows_to_stage(x3_ref, stage)
            h_scr[...] = _stage_load(stage).astype(BF16)
            stage[...] = jnp.zeros_like(stage)

        h = h_scr[...]
        gate = jnp.dot(h, wg_ref[0], preferred_element_type=F32)
        up = jnp.dot(h, wu_ref[0], preferred_element_type=F32)
        act = (_silu(gate) * up).astype(BF16)
        _stage_store(stage, jnp.dot(act, wd_ref[0], preferred_element_type=F32), accumulate=True)

        @pl.when(j == pl.num_programs(1) - 1)
        def _():
            _stage_to_rows(stage, y3_ref)


def _experts(tile_expert, n_used, xs3, w_gate, w_up, w_down, *, tmg, tf):
    n_rows = xs3.shape[0]
    E, D, F = w_gate.shape
    nj = F // tf
    row_blk = lambda i, j, te, nu: (jnp.minimum(i, nu[0] - 1), 0, 0)

    def chunk(i, j, nu):
        serp = lambda ii, jj: jnp.where(ii % 2 == 0, jj, nj - 1 - jj)
        return jnp.where(i < nu[0], serp(i, j), serp(nu[0] - 1, nj - 1))

    return pl.pallas_call(
        _expert_kernel,
        grid_spec=pltpu.PrefetchScalarGridSpec(
            num_scalar_prefetch=2,
            grid=(n_rows // tmg, nj),
            in_specs=[
                pl.BlockSpec((tmg, ROW_SUB, LANES), row_blk),
                pl.BlockSpec((1, D, tf), lambda i, j, te, nu: (te[i], 0, chunk(i, j, nu))),
                pl.BlockSpec((1, D, tf), lambda i, j, te, nu: (te[i], 0, chunk(i, j, nu))),
                pl.BlockSpec((1, tf, D), lambda i, j, te, nu: (te[i], chunk(i, j, nu), 0)),
            ],
            out_specs=pl.BlockSpec((tmg, ROW_SUB, LANES), lambda i, j, te, nu: (i, 0, 0)),
            scratch_shapes=[pltpu.VMEM((tmg // SUBLANES, ROW_SUB * SUBLANES, LANES), F32),
                            pltpu.VMEM((tmg, D), BF16)],
        ),
        out_shape=jax.ShapeDtypeStruct((n_rows, ROW_SUB, LANES), F32),
        compiler_params=pltpu.CompilerParams(
            dimension_semantics=("arbitrary", "arbitrary"), vmem_limit_bytes=VMEM_LIMIT),
        name="experts",
    )(tile_expert, n_used, xs3, w_gate.astype(BF16), w_up.astype(BF16), w_down.astype(BF16))


def _combine_kernel(pos1_ref, pos2_ref, x_ref, wts_ref, fg_ref, y3_ref, o_ref, rows, stage, sems):
    tm = x_ref.shape[0]
    i = pl.program_id(0)

    def issue(tile, slot):
        base = tile * tm

        def body(blk, carry):
            for u in range(DMA_UNROLL):
                t = blk * DMA_UNROLL + u
                for k, pos_ref in enumerate((pos1_ref, pos2_ref)):
                    pltpu.make_async_copy(y3_ref.at[pos_ref[base + t]], rows.at[slot, k, t],
                                          sems.at[slot]).start(priority=k)
            return carry

        lax.fori_loop(0, tm // DMA_UNROLL, body, 0)

    @pl.when(i == 0)
    def _():
        issue(0, 0)

    @pl.when(i + 1 < pl.num_programs(0))
    def _():
        issue(i + 1, (i + 1) % 2)

    slot = i % 2
    for k in range(2):
        pltpu.make_async_copy(y3_ref.at[pl.ds(0, tm)], rows.at[slot, k], sems.at[slot]).wait()

    w = wts_ref[...]
    _rows_to_stage(rows.at[slot, 0], stage)
    y1 = _stage_load(stage) * w[:, 0:1]
    _rows_to_stage(rows.at[slot, 1], stage)
    y2 = _stage_load(stage) * w[:, 1:2]
    o_ref[...] = _rms(x_ref[...] + (y1 + y2), fg_ref[...])


def _combine(pos1, pos2, x, wts, final_g, y3, *, tm):
    T, D = x.shape
    return pl.pallas_call(
        _combine_kernel,
        grid_spec=pltpu.PrefetchScalarGridSpec(
            num_scalar_prefetch=2,
            grid=(T // tm,),
            in_specs=[
                pl.BlockSpec((tm, D), lambda i, *_: (i, 0)),
                pl.BlockSpec((tm, LANES), lambda i, *_: (i, 0)),
                pl.BlockSpec((1, D), lambda i, *_: (0, 0)),
                pl.BlockSpec(memory_space=pl.ANY),
            ],
            out_specs=pl.BlockSpec((tm, D), lambda i, *_: (i, 0)),
            scratch_shapes=[pltpu.VMEM((2, 2, tm, ROW_SUB, LANES), F32),
                            pltpu.VMEM((tm // SUBLANES, ROW_SUB * SUBLANES, LANES), F32),
                            pltpu.SemaphoreType.DMA((2,))],
        ),
        out_shape=jax.ShapeDtypeStruct((T, D), F32),
        compiler_params=pltpu.CompilerParams(dimension_semantics=("arbitrary",), vmem_limit_bytes=VMEM_LIMIT),
        name="combine",
    )(pos1, pos2, x, wts, final_g.reshape(1, D), y3)


def _moe(x, g, w_router, w_gate, w_up, w_down, final_g, *, tm, tmg, tf):
    T, D = x.shape
    E = N_EXPERTS
    h3, keys, wts, cnt = _router(x, g, w_router, tm=tm)
    key1, key2 = keys[:, 0], keys[:, 1]

    counts = cnt[0, :E]
    tiles = (counts + (tmg - 1)) // tmg
    tile_end = jnp.cumsum(tiles)
    starts = ((tile_end - tiles) * tmg).astype(I32)
    n_tiles = 2 * T // tmg + E
    n_used = tile_end[-1:].astype(I32)
    last_expert = jnp.max(jnp.where(tiles > 0, jnp.arange(E), 0))
    tile_expert = jnp.sum(jnp.arange(n_tiles)[:, None] >= tile_end[None, :], axis=1)
    tile_expert = jnp.minimum(tile_expert, last_expert).astype(I32)

    pos1 = starts[key1 >> KEY_SHIFT] + (key1 & KEY_MASK)
    pos2 = starts[key2 >> KEY_SHIFT] + (key2 & KEY_MASK)

    pad_start = (starts + counts).astype(I32)
    pad_len = (tiles * tmg - counts).astype(I32)
    xs3 = _dispatch(pos1, pos2, pad_start, pad_len, n_used, h3, n_tiles * tmg, tm=tm, tmg=tmg)
    y3 = _experts(tile_expert, n_used, xs3, w_gate, w_up, w_down, tmg=tmg, tf=tf)
    return _combine(pos1, pos2, x, wts, final_g, y3, tm=tm)


def kernel(x, attn_norm, ffn_norm, w_in, b_in, conv_w, conv_b, conv_ln_g, conv_ln_b, sinks, w_out, b_out,
           ffn_w_gate, ffn_w_up, ffn_w_down, moe_router, moe_w_gate, moe_w_up, moe_w_down, final_norm):
    B, S, D = x.shape
    ts = min(S, 512)
    tm = min(B * S, 512)

    def mixer(x, l):
        return _mixer(x, attn_norm[l], w_in[l], b_in[l], conv_w[l], conv_b[l], conv_ln_g[l], conv_ln_b[l],
                      sinks[l], w_out[l], b_out[l], ts=ts)

    x = mixer(x, 0)
    x = _ffn(x.reshape(B * S, D), ffn_norm[0], ffn_w_gate[0], ffn_w_up[0], ffn_w_down[0],
             tm=tm)
    x = mixer(x.reshape(B, S, D), 1)
    x = _moe(x.reshape(B * S, D), ffn_norm[1], moe_router[0], moe_w_gate[0], moe_w_up[0], moe_w_down[0],
             final_norm, tm=tm, tmg=tm, tf=moe_w_gate.shape[3] // 2)
    return x.reshape(B, S, D)
```

```python
import functools

import jax
import jax.numpy as jnp
from jax import lax
from jax.experimental import pallas as pl
from jax.experimental.pallas import tpu as pltpu

F32 = jnp.float32
BF16 = jnp.bfloat16
I32 = jnp.int32

D_MODEL = 1024
CONV_CH = 512
CONV_KERNEL = 31
HEAD_DIM = 64
N_Q_HEADS = 8
N_KV_HEADS = 2
ATTN_WIDTH = N_Q_HEADS * HEAD_DIM
KV_WIDTH = N_KV_HEADS * HEAD_DIM
BLOCK = 128
N_EXPERTS = 8
EPS = 1e-5

LANES = 128
SUBLANES = 8
CONV_HALO = 32
CONV_ROWS = 32
VMEM_LIMIT = 56 * 1024 * 1024

COL_Q = 2 * CONV_CH
COL_K = COL_Q + ATTN_WIDTH
COL_V = COL_K + KV_WIDTH
IN_COLS = COL_V + KV_WIDTH

ROW_SUB = D_MODEL // LANES
KEY_SHIFT = 20
KEY_MASK = (1 << KEY_SHIFT) - 1
DMA_UNROLL = 8


def _rms(x, g):
    ms = jnp.mean(x * x, axis=-1, keepdims=True)
    return x * lax.rsqrt(ms + EPS) * g


def _silu(x):
    return x * jax.nn.sigmoid(x)


def _mixer_kernel(x_ref, g_ref, win_ref, bin_ref, cw_ref, cb_ref, lng_ref, lnb_ref,
                  sink_ref, wout_ref, bout_ref, o_ref, halo, sbuf, kbuf, vbuf, ybuf, *, ts):
    first = pl.program_id(1) == 0
    n_chunks = CONV_CH // LANES
    n_blocks = ts // BLOCK

    def store_u(row0, val):
        for c in range(n_chunks):
            sbuf[c, pl.ds(2 * row0, val.shape[0], stride=2), :] = val[:, c * LANES:(c + 1) * LANES]

    store_u(0, jnp.where(first, 0.0, halo[...]))
    tap0 = CONV_HALO - (CONV_KERNEL - 1)

    def conv_rows(r0):
        accs = [jnp.broadcast_to(cb_ref[:, c * LANES:(c + 1) * LANES], (CONV_ROWS, LANES)) for c in range(n_chunks)]
        for j in range(CONV_KERNEL):
            for c in range(n_chunks):
                accs[c] = accs[c] + (cw_ref[j:j + 1, c * LANES:(c + 1) * LANES]
                                     * sbuf[c, pl.ds(2 * (r0 + tap0 + j), CONV_ROWS, stride=2), :])
        acc = jnp.concatenate(accs, axis=1)
        mu = jnp.mean(acc, axis=-1, keepdims=True)
        xc = acc - mu
        var = jnp.mean(xc * xc, axis=-1, keepdims=True)
        yn = xc * lax.rsqrt(var + EPS) * lng_ref[...] + lnb_ref[...]
        ybuf[r0:r0 + CONV_ROWS, 0:CONV_CH] = _silu(yn).astype(BF16)

    for buf in (kbuf, vbuf):
        buf[0:BLOCK, :] = jnp.where(first, jnp.zeros((BLOCK, 4 * LANES), BF16), buf[ts:ts + BLOCK, :])

    low_o = lax.broadcasted_iota(I32, (BLOCK, LANES), 1) < HEAD_DIM

    def store_kv(n, kv):
        rows = slice((n + 1) * BLOCK, (n + 2) * BLOCK)
        for src, buf in ((kv[:, 0:LANES], kbuf), (kv[:, LANES:2 * LANES], vbuf)):
            swapped = pltpu.roll(src, HEAD_DIM, axis=1)
            zero = jnp.zeros_like(src)
            buf[rows, 0 * LANES:1 * LANES] = jnp.where(low_o, src, zero).astype(BF16)
            buf[rows, 1 * LANES:2 * LANES] = jnp.where(low_o, zero, swapped).astype(BF16)
            buf[rows, 2 * LANES:3 * LANES] = jnp.where(low_o, swapped, zero).astype(BF16)
            buf[rows, 3 * LANES:4 * LANES] = jnp.where(low_o, zero, src).astype(BF16)

    qi = lax.broadcasted_iota(I32, (BLOCK, 2 * BLOCK), 0)
    kj = lax.broadcasted_iota(I32, (BLOCK, 2 * BLOCK), 1)
    band = (kj > qi) & (kj <= qi + BLOCK)
    band_first = band & ((kj >= BLOCK) | jnp.logical_not(first))

    def attend(n, q):
        rows = slice(n * BLOCK, (n + 2) * BLOCK)
        mask = band_first if n == 0 else band
        for hkv in range(N_KV_HEADS):
            k_bd = jnp.concatenate([kbuf[rows, (2 * hkv) * LANES:(2 * hkv + 1) * LANES],
                                    kbuf[rows, (2 * hkv + 1) * LANES:(2 * hkv + 2) * LANES]], axis=0)
            v_bd = jnp.concatenate([vbuf[rows, (2 * hkv) * LANES:(2 * hkv + 1) * LANES],
                                    vbuf[rows, (2 * hkv + 1) * LANES:(2 * hkv + 2) * LANES]], axis=0)
            for pair in range(2):
                hp = 2 * hkv + pair
                qp = q[:, hp * LANES:(hp + 1) * LANES]
                s = lax.dot_general(qp, k_bd, (((1,), (1,)), ((), ())),
                                    preferred_element_type=F32)
                ps, rden = [], []
                for hh in range(2):
                    sink = sink_ref[2 * hp + hh]
                    sh = jnp.where(mask, s[:, hh * 2 * BLOCK:(hh + 1) * 2 * BLOCK], -jnp.inf)
                    m = jnp.maximum(jnp.max(sh, axis=-1, keepdims=True), sink)
                    p = jnp.exp(sh - m)
                    den = jnp.sum(p, axis=-1, keepdims=True) + jnp.exp(sink - m)
                    ps.append(p.astype(BF16))
                    rden.append(1.0 / den)
                o = jnp.dot(jnp.concatenate(ps, axis=1), v_bd, preferred_element_type=F32)
                o = o * jnp.where(low_o, rden[0], rden[1])
                ybuf[n * BLOCK:(n + 1) * BLOCK, CONV_CH + hp * LANES:CONV_CH + (hp + 1) * LANES] = o.astype(BF16)

    x = x_ref[0]
    h = _rms(x, g_ref[...]).astype(BF16)
    ag = jnp.dot(h, win_ref[:, 0:COL_Q], preferred_element_type=F32) + bin_ref[:, 0:COL_Q]
    u = ag[:, :CONV_CH] * jax.nn.sigmoid(ag[:, CONV_CH:])
    store_u(CONV_HALO, u)
    halo[...] = u[ts - CONV_HALO:, :]
    q = jnp.dot(h, win_ref[:, COL_Q:COL_K], preferred_element_type=F32) + bin_ref[:, COL_Q:COL_K]
    q = (q * (HEAD_DIM ** -0.5)).astype(BF16)
    kv = jnp.dot(h, win_ref[:, COL_K:IN_COLS], preferred_element_type=F32) + bin_ref[:, COL_K:IN_COLS]

    for n in range(n_blocks):
        blk = slice(n * BLOCK, (n + 1) * BLOCK)
        store_kv(n, kv[blk, :])
        attend(n, q[blk, :])
        for r0 in range(n * BLOCK, (n + 1) * BLOCK, CONV_ROWS):
            conv_rows(r0)
        y = jnp.dot(ybuf[blk, :], wout_ref[...], preferred_element_type=F32)
        o_ref[0, blk, :] = x[blk, :] + y + bout_ref[...]


def _mixer(x, g, w_in, b_in, conv_w, conv_b, ln_g, ln_b, sinks, w_out, b_out, *, ts):
    B, S, D = x.shape
    row = lambda a: a.reshape(1, -1)
    const = lambda shape: pl.BlockSpec(shape, lambda b, s: (0,) * len(shape))
    return pl.pallas_call(
        functools.partial(_mixer_kernel, ts=ts),
        grid=(B, S // ts),
        in_specs=[
            pl.BlockSpec((1, ts, D), lambda b, s: (b, s, 0)),
            const((1, D)),
            const((D, IN_COLS)),
            const((1, IN_COLS)),
            const((CONV_KERNEL, CONV_CH)),
            const((1, CONV_CH)),
            const((1, CONV_CH)),
            const((1, CONV_CH)),
            pl.BlockSpec(memory_space=pltpu.SMEM),
            const((D, D)),
            const((1, D)),
        ],
        out_specs=pl.BlockSpec((1, ts, D), lambda b, s: (b, s, 0)),
        out_shape=jax.ShapeDtypeStruct((B, S, D), F32),
        scratch_shapes=[
            pltpu.VMEM((CONV_HALO, CONV_CH), F32),
            pltpu.VMEM((CONV_CH // LANES, 2 * (CONV_HALO + ts), LANES), F32),
            pltpu.VMEM((BLOCK + ts, 4 * LANES), BF16),
            pltpu.VMEM((BLOCK + ts, 4 * LANES), BF16),
            pltpu.VMEM((ts, D), BF16),
        ],
        compiler_params=pltpu.CompilerParams(
            dimension_semantics=("arbitrary", "arbitrary"), vmem_limit_bytes=VMEM_LIMIT),
        name="mixer",
    )(x, row(g), w_in.astype(BF16), row(b_in), conv_w, row(conv_b), row(ln_g), row(ln_b),
      sinks, w_out.astype(BF16), row(b_out))


def _ffn_kernel(x_ref, g_ref, wg_ref, wu_ref, wd_ref, o_ref):
    x = x_ref[...]
    h = _rms(x, g_ref[...]).astype(BF16)
    gate = jnp.dot(h, wg_ref[...], preferred_element_type=F32)
    up = jnp.dot(h, wu_ref[...], preferred_element_type=F32)
    a = (_silu(gate) * up).astype(BF16)
    o_ref[...] = x + jnp.dot(a, wd_ref[...], preferred_element_type=F32)


def _ffn(x, g, w_gate, w_up, w_down, *, tm):
    T, D = x.shape
    F = w_gate.shape[1]
    resident = lambda shape: pl.BlockSpec(shape, lambda i: (0, 0), pipeline_mode=pl.Buffered(1))
    return pl.pallas_call(
        _ffn_kernel,
        grid=(T // tm,),
        in_specs=[
            pl.BlockSpec((tm, D), lambda i: (i, 0)),
            pl.BlockSpec((1, D), lambda i: (0, 0)),
            resident((D, F)),
            resident((D, F)),
            resident((F, D)),
        ],
        out_specs=pl.BlockSpec((tm, D), lambda i: (i, 0)),
        out_shape=jax.ShapeDtypeStruct((T, D), F32),
        compiler_params=pltpu.CompilerParams(dimension_semantics=("arbitrary",), vmem_limit_bytes=VMEM_LIMIT),
        name="dense_ffn",
    )(x, g.reshape(1, D), w_gate.astype(BF16), w_up.astype(BF16), w_down.astype(BF16))


def _stage_store(stage_ref, val, accumulate=False):
    n = val.shape[0]
    for c in range(ROW_SUB):
        tile = val[:, c * LANES:(c + 1) * LANES].reshape(n // SUBLANES, SUBLANES, LANES)
        if accumulate:
            stage_ref[:, c * SUBLANES:(c + 1) * SUBLANES, :] += tile
        else:
            stage_ref[:, c * SUBLANES:(c + 1) * SUBLANES, :] = tile


def _stage_load(stage_ref):
    n = stage_ref.shape[0] * SUBLANES
    return jnp.concatenate(
        [stage_ref[:, c * SUBLANES:(c + 1) * SUBLANES, :].reshape(n, LANES) for c in range(ROW_SUB)], axis=1)


def _stage_to_rows(stage_ref, rows_ref):
    for g in range(stage_ref.shape[0]):
        for t in range(SUBLANES):
            rows_ref[g * SUBLANES + t] = stage_ref[g, pl.ds(t, ROW_SUB, stride=SUBLANES), :]


def _rows_to_stage(rows_ref, stage_ref):
    for g in range(stage_ref.shape[0]):
        for t in range(SUBLANES):
            stage_ref[g, pl.ds(t, ROW_SUB, stride=SUBLANES), :] = rows_ref[g * SUBLANES + t]


def _split_bf16(a):
    hi = a.astype(BF16)
    lo = (a - hi.astype(F32)).astype(BF16)
    return hi, lo


def _router_kernel(x_ref, g_ref, wr_hi_ref, wr_lo_ref, h3_ref, keys_ref, wts_ref, cnt_ref, stage, count):
    tm = x_ref.shape[0]
    lane = lax.broadcasted_iota(I32, (tm, LANES), 1)

    @pl.when(pl.program_id(0) == 0)
    def _():
        count[...] = jnp.zeros_like(count)

    hf = _rms(x_ref[...], g_ref[...])
    h_hi, h_lo = _split_bf16(hf)
    logits = (jnp.dot(h_hi, wr_hi_ref[...], preferred_element_type=F32)
              + jnp.dot(h_lo, wr_hi_ref[...], preferred_element_type=F32)
              + jnp.dot(h_hi, wr_lo_ref[...], preferred_element_type=F32))
    lg = jnp.where(lane < N_EXPERTS, logits, -jnp.inf)
    v1 = jnp.max(lg, axis=-1, keepdims=True)
    i1 = jnp.min(jnp.where(lg == v1, lane, LANES), axis=-1, keepdims=True)
    lg2 = jnp.where(lane == i1, -jnp.inf, lg)
    v2 = jnp.max(lg2, axis=-1, keepdims=True)
    i2 = jnp.min(jnp.where(lg2 == v2, lane, LANES), axis=-1, keepdims=True)
    t = jnp.exp(v2 - v1)
    w1 = 1.0 / (1.0 + t)
    w2 = t / (1.0 + t)

    onehot = jnp.where((lane == i1) | (lane == i2), 1.0, 0.0)
    r = lax.broadcasted_iota(I32, (tm, tm), 0)
    c = lax.broadcasted_iota(I32, (tm, tm), 1)
    earlier = jnp.where(c < r, 1.0, 0.0).astype(BF16)
    before = count[...] + jnp.dot(earlier, onehot.astype(BF16), preferred_element_type=F32)
    rank1 = jnp.sum(jnp.where(lane == i1, before, 0.0), axis=-1, keepdims=True).astype(I32)
    rank2 = jnp.sum(jnp.where(lane == i2, before, 0.0), axis=-1, keepdims=True).astype(I32)
    count[...] += jnp.sum(onehot, axis=0, keepdims=True)

    key1 = i1 * (1 << KEY_SHIFT) + rank1
    key2 = i2 * (1 << KEY_SHIFT) + rank2
    keys_ref[...] = jnp.where(lane == 0, key1, jnp.where(lane == 1, key2, 0))
    wts_ref[...] = jnp.where(lane == 0, w1, jnp.where(lane == 1, w2, 0.0))
    cnt_ref[...] = count[...].astype(I32)
    _stage_store(stage, hf)
    _stage_to_rows(stage, h3_ref)


def _router(x, g, w_router, *, tm):
    T, D = x.shape
    wr = jnp.pad(w_router, ((0, 0), (0, LANES - N_EXPERTS)))
    wr_hi = wr.astype(BF16)
    wr_lo = (wr - wr_hi.astype(F32)).astype(BF16)
    return pl.pallas_call(
        _router_kernel,
        grid=(T // tm,),
        in_specs=[
            pl.BlockSpec((tm, D), lambda i: (i, 0)),
            pl.BlockSpec((1, D), lambda i: (0, 0)),
            pl.BlockSpec((D, LANES), lambda i: (0, 0)),
            pl.BlockSpec((D, LANES), lambda i: (0, 0)),
        ],
        out_specs=[
            pl.BlockSpec((tm, ROW_SUB, LANES), lambda i: (i, 0, 0)),
            pl.BlockSpec((tm, LANES), lambda i: (i, 0)),
            pl.BlockSpec((tm, LANES), lambda i: (i, 0)),
            pl.BlockSpec((1, LANES), lambda i: (0, 0)),
        ],
        out_shape=[
            jax.ShapeDtypeStruct((T, ROW_SUB, LANES), F32),
            jax.ShapeDtypeStruct((T, LANES), I32),
            jax.ShapeDtypeStruct((T, LANES), F32),
            jax.ShapeDtypeStruct((1, LANES), I32),
        ],
        scratch_shapes=[pltpu.VMEM((tm // SUBLANES, ROW_SUB * SUBLANES, LANES), F32),
                        pltpu.VMEM((1, LANES), F32)],
        compiler_params=pltpu.CompilerParams(dimension_semantics=("arbitrary",), vmem_limit_bytes=VMEM_LIMIT),
        name="router",
    )(x, g.reshape(1, D), wr_hi, wr_lo)


def _row_position(key_ref, starts_ref, t):
    key = key_ref[t]
    return starts_ref[key >> KEY_SHIFT] + (key & KEY_MASK)


def _dispatch_kernel(key1_ref, key2_ref, starts_ref, pad_start_ref, pad_len_ref, nu_ref, h3_ref, xs_ref,
                     zeros, sem, zsem, *, tmg):
    tm = h3_ref.shape[0]
    base = pl.program_id(0) * tm

    @pl.when(pl.program_id(0) == 0)
    def _():
        zeros[...] = jnp.zeros_like(zeros)

        def pad_copies(act):
            for e in range(N_EXPERTS):
                start, length = pad_start_ref[e], pad_len_ref[e]
                for b in range(tmg.bit_length() - 1):
                    size = 1 << b
                    offset = start + ((length >> (b + 1)) << (b + 1))
                    copy = pltpu.make_async_copy(zeros.at[pl.ds(0, size)], xs_ref.at[pl.ds(offset, size)], zsem)
                    pl.when(((length >> b) & 1) == 1)(lambda copy=copy: act(copy))

        def tile_copies(act):
            def body(i, carry):
                act(pltpu.make_async_copy(zeros, xs_ref.at[pl.ds(i * tmg, tmg)], zsem))
                return carry

            lax.fori_loop(nu_ref[0], xs_ref.shape[0] // tmg, body, 0)

        for act in (lambda c: c.start(), lambda c: c.wait()):
            pad_copies(act)
            tile_copies(act)

    def issue(blk, carry):
        for u in range(DMA_UNROLL):
            t = blk * DMA_UNROLL + u
            for k, key_ref in enumerate((key1_ref, key2_ref)):
                pos = _row_position(key_ref, starts_ref, base + t)
                pltpu.make_async_copy(h3_ref.at[t], xs_ref.at[pos], sem).start(priority=k)
        return carry

    lax.fori_loop(0, tm // DMA_UNROLL, issue, 0)
    for _ in range(2):
        pltpu.make_async_copy(h3_ref, xs_ref.at[pl.ds(0, tm)], sem).wait()


def _dispatch(key1, key2, starts, pad_start, pad_len, n_used, h3, n_rows, *, tm, tmg):
    T = h3.shape[0]
    return pl.pallas_call(
        functools.partial(_dispatch_kernel, tmg=tmg),
        grid_spec=pltpu.PrefetchScalarGridSpec(
            num_scalar_prefetch=6,
            grid=(T // tm,),
            in_specs=[pl.BlockSpec((tm, ROW_SUB, LANES), lambda i, *_: (i, 0, 0))],
            out_specs=pl.BlockSpec(memory_space=pl.ANY),
            scratch_shapes=[pltpu.VMEM((tmg, ROW_SUB, LANES), F32),
                            pltpu.SemaphoreType.DMA, pltpu.SemaphoreType.DMA],
        ),
        out_shape=jax.ShapeDtypeStruct((n_rows, ROW_SUB, LANES), F32),
        compiler_params=pltpu.CompilerParams(dimension_semantics=("arbitrary",)),
        name="dispatch",
    )(key1, key2, starts, pad_start, pad_len, n_used, h3)


def _expert_kernel(te_ref, nu_ref, x3_ref, wg_ref, wu_ref, wd_ref, y3_ref, stage, h_scr):
    del te_ref
    j = pl.program_id(1)
    used = pl.program_id(0) < nu_ref[0]

    @pl.when(jnp.logical_not(used) & (j == 0))
    def _():
        y3_ref[...] = jnp.zeros_like(y3_ref)

    @pl.when(used)
    def _():
        @pl.when(j == 0)
        def _():
            _rows_to_stage(x3_ref, stage)
            h_scr[...] = _stage_load(stage).astype(BF16)
            stage[...] = jnp.zeros_like(stage)

        h = h_scr[...]
        gate = jnp.dot(h, wg_ref[0], preferred_element_type=F32)
        up = jnp.dot(h, wu_ref[0], preferred_element_type=F32)
        act = (_silu(gate) * up).astype(BF16)
        _stage_store(stage, jnp.dot(act, wd_ref[0], preferred_element_type=F32), accumulate=True)

        @pl.when(j == pl.num_programs(1) - 1)
        def _():
            _stage_to_rows(stage, y3_ref)


def _experts(tile_end, n_used, xs3, w_gate, w_up, w_down, *, tmg, tf):
    n_rows = xs3.shape[0]
    E, D, F = w_gate.shape
    nj = F // tf
    row_blk = lambda i, j, te, nu: (jnp.minimum(i, nu[0] - 1), 0, 0)

    def expert(i, te, nu):
        ii = jnp.minimum(i, nu[0] - 1)
        return sum((ii >= te[e]).astype(I32) for e in range(E - 1))

    def chunk(i, j, nu):
        serp = lambda ii, jj: jnp.where(ii % 2 == 0, jj, nj - 1 - jj)
        return jnp.where(i < nu[0], serp(i, j), serp(nu[0] - 1, nj - 1))

    return pl.pallas_call(
        _expert_kernel,
        grid_spec=pltpu.PrefetchScalarGridSpec(
            num_scalar_prefetch=2,
            grid=(n_rows // tmg, nj),
            in_specs=[
                pl.BlockSpec((tmg, ROW_SUB, LANES), row_blk),
                pl.BlockSpec((1, D, tf), lambda i, j, te, nu: (expert(i, te, nu), 0, chunk(i, j, nu))),
                pl.BlockSpec((1, D, tf), lambda i, j, te, nu: (expert(i, te, nu), 0, chunk(i, j, nu))),
                pl.BlockSpec((1, tf, D), lambda i, j, te, nu: (expert(i, te, nu), chunk(i, j, nu), 0)),
            ],
            out_specs=pl.BlockSpec((tmg, ROW_SUB, LANES), lambda i, j, te, nu: (i, 0, 0)),
            scratch_shapes=[pltpu.VMEM((tmg // SUBLANES, ROW_SUB * SUBLANES, LANES), F32),
                            pltpu.VMEM((tmg, D), BF16)],
        ),
        out_shape=jax.ShapeDtypeStruct((n_rows, ROW_SUB, LANES), F32),
        compiler_params=pltpu.CompilerParams(
            dimension_semantics=("arbitrary", "arbitrary"), vmem_limit_bytes=VMEM_LIMIT),
        name="experts",
    )(tile_end, n_used, xs3, w_gate.astype(BF16), w_up.astype(BF16), w_down.astype(BF16))


def _combine_kernel(key1_ref, key2_ref, starts_ref, x_ref, wts_ref, fg_ref, y3_ref, o_ref, rows, stage, sems):
    tm = x_ref.shape[0]
    i = pl.program_id(0)

    def issue(tile, slot):
        base = tile * tm

        def body(blk, carry):
            for u in range(DMA_UNROLL):
                t = blk * DMA_UNROLL + u
                for k, key_ref in enumerate((key1_ref, key2_ref)):
                    pos = _row_position(key_ref, starts_ref, base + t)
                    pltpu.make_async_copy(y3_ref.at[pos], rows.at[slot, k, t],
                                          sems.at[slot]).start(priority=k)
            return carry

        lax.fori_loop(0, tm // DMA_UNROLL, body, 0)

    @pl.when(i == 0)
    def _():
        issue(0, 0)

    @pl.when(i + 1 < pl.num_programs(0))
    def _():
        issue(i + 1, (i + 1) % 2)

    slot = i % 2
    for k in range(2):
        pltpu.make_async_copy(y3_ref.at[pl.ds(0, tm)], rows.at[slot, k], sems.at[slot]).wait()

    w = wts_ref[...]
    _rows_to_stage(rows.at[slot, 0], stage)
    y1 = _stage_load(stage) * w[:, 0:1]
    _rows_to_stage(rows.at[slot, 1], stage)
    y2 = _stage_load(stage) * w[:, 1:2]
    o_ref[...] = _rms(x_ref[...] + (y1 + y2), fg_ref[...])


def _combine(key1, key2, starts, x, wts, final_g, y3, *, tm):
    T, D = x.shape
    return pl.pallas_call(
        _combine_kernel,
        grid_spec=pltpu.PrefetchScalarGridSpec(
            num_scalar_prefetch=3,
            grid=(T // tm,),
            in_specs=[
                pl.BlockSpec((tm, D), lambda i, *_: (i, 0)),
                pl.BlockSpec((tm, LANES), lambda i, *_: (i, 0)),
                pl.BlockSpec((1, D), lambda i, *_: (0, 0)),
                pl.BlockSpec(memory_space=pl.ANY),
            ],
            out_specs=pl.BlockSpec((tm, D), lambda i, *_: (i, 0)),
            scratch_shapes=[pltpu.VMEM((2, 2, tm, ROW_SUB, LANES), F32),
                            pltpu.VMEM((tm // SUBLANES, ROW_SUB * SUBLANES, LANES), F32),
                            pltpu.SemaphoreType.DMA((2,))],
        ),
        out_shape=jax.ShapeDtypeStruct((T, D), F32),
        compiler_params=pltpu.CompilerParams(dimension_semantics=("arbitrary",), vmem_limit_bytes=VMEM_LIMIT),
        name="combine",
    )(key1, key2, starts, x, wts, final_g.reshape(1, D), y3)


def _moe(x, g, w_router, w_gate, w_up, w_down, final_g, *, tm, tmg, tf):
    T, D = x.shape
    E = N_EXPERTS
    h3, keys, wts, cnt = _router(x, g, w_router, tm=tm)
    key1, key2 = keys[:, 0], keys[:, 1]

    counts = cnt[0, :E]
    tiles = (counts + (tmg - 1)) // tmg
    tile_end = jnp.cumsum(tiles).astype(I32)
    starts = ((tile_end - tiles) * tmg).astype(I32)
    n_tiles = 2 * T // tmg + E
    n_used = tile_end[-1:]
    pad_start = (starts + counts).astype(I32)
    pad_len = (tiles * tmg - counts).astype(I32)

    xs3 = _dispatch(key1, key2, starts, pad_start, pad_len, n_used, h3, n_tiles * tmg, tm=tm, tmg=tmg)
    y3 = _experts(tile_end, n_used, xs3, w_gate, w_up, w_down, tmg=tmg, tf=tf)
    return _combine(key1, key2, starts, x, wts, final_g, y3, tm=tm)


def kernel(x, attn_norm, ffn_norm, w_in, b_in, conv_w, conv_b, conv_ln_g, conv_ln_b, sinks, w_out, b_out,
           ffn_w_gate, ffn_w_up, ffn_w_down, moe_router, moe_w_gate, moe_w_up, moe_w_down, final_norm):
    B, S, D = x.shape
    ts = min(S, 512)
    tm = min(B * S, 512)

    def mixer(x, l):
        return _mixer(x, attn_norm[l], w_in[l], b_in[l], conv_w[l], conv_b[l], conv_ln_g[l], conv_ln_b[l],
                      sinks[l], w_out[l], b_out[l], ts=ts)

    x = mixer(x, 0)
    x = _ffn(x.reshape(B * S, D), ffn_norm[0], ffn_w_gate[0], ffn_w_up[0], ffn_w_down[0],
             tm=tm)
    x = mixer(x.reshape(B, S, D), 1)
    x = _moe(x.reshape(B * S, D), ffn_norm[1], moe_router[0], moe_w_gate[0], moe_w_up[0], moe_w_down[0],
             final_norm, tm=tm, tmg=tm, tf=moe_w_gate.shape[3] // 2)
    return x.reshape(B, S, D)
```

```python
import functools

import jax
import jax.numpy as jnp
from jax import lax
from jax.experimental import pallas as pl
from jax.experimental.pallas import tpu as pltpu

F32 = jnp.float32
BF16 = jnp.bfloat16
I32 = jnp.int32

D_MODEL = 1024
CONV_CH = 512
CONV_KERNEL = 31
HEAD_DIM = 64
N_Q_HEADS = 8
N_KV_HEADS = 2
ATTN_WIDTH = N_Q_HEADS * HEAD_DIM
KV_WIDTH = N_KV_HEADS * HEAD_DIM
BLOCK = 128
N_EXPERTS = 8
EPS = 1e-5

LANES = 128
SUBLANES = 8
CONV_HALO = 32
CONV_ROWS = 32
VMEM_LIMIT = 56 * 1024 * 1024

COL_Q = 2 * CONV_CH
COL_K = COL_Q + ATTN_WIDTH
COL_V = COL_K + KV_WIDTH
IN_COLS = COL_V + KV_WIDTH

ROW_SUB = D_MODEL // LANES
KEY_SHIFT = 20
KEY_MASK = (1 << KEY_SHIFT) - 1
DMA_UNROLL = 8


def _rms(x, g):
    ms = jnp.mean(x * x, axis=-1, keepdims=True)
    return x * lax.rsqrt(ms + EPS) * g


def _silu(x):
    return x * jax.nn.sigmoid(x)


def _mixer_kernel(x_ref, g_ref, win_ref, bin_ref, cw_ref, cb_ref, lng_ref, lnb_ref,
                  sink_ref, wout_ref, bout_ref, o_ref, halo, sbuf, kbuf, vbuf, ybuf, *, ts):
    first = pl.program_id(1) == 0
    n_chunks = CONV_CH // LANES
    n_blocks = ts // BLOCK

    def store_u(row0, val):
        for c in range(n_chunks):
            sbuf[c, pl.ds(2 * row0, val.shape[0], stride=2), :] = val[:, c * LANES:(c + 1) * LANES]

    store_u(0, jnp.where(first, 0.0, halo[...]))
    tap0 = CONV_HALO - (CONV_KERNEL - 1)

    def conv_rows(r0):
        accs = [jnp.broadcast_to(cb_ref[:, c * LANES:(c + 1) * LANES], (CONV_ROWS, LANES)) for c in range(n_chunks)]
        for j in range(CONV_KERNEL):
            for c in range(n_chunks):
                accs[c] = accs[c] + (cw_ref[j:j + 1, c * LANES:(c + 1) * LANES]
                                     * sbuf[c, pl.ds(2 * (r0 + tap0 + j), CONV_ROWS, stride=2), :])
        acc = jnp.concatenate(accs, axis=1)
        mu = jnp.mean(acc, axis=-1, keepdims=True)
        xc = acc - mu
        var = jnp.mean(xc * xc, axis=-1, keepdims=True)
        yn = xc * lax.rsqrt(var + EPS) * lng_ref[...] + lnb_ref[...]
        ybuf[r0:r0 + CONV_ROWS, 0:CONV_CH] = _silu(yn).astype(BF16)

    for buf in (kbuf, vbuf):
        buf[0:BLOCK, :] = jnp.where(first, jnp.zeros((BLOCK, 4 * LANES), BF16), buf[ts:ts + BLOCK, :])

    low_o = lax.broadcasted_iota(I32, (BLOCK, LANES), 1) < HEAD_DIM

    def store_kv(n, kv):
        rows = slice((n + 1) * BLOCK, (n + 2) * BLOCK)
        for src, buf in ((kv[:, 0:LANES], kbuf), (kv[:, LANES:2 * LANES], vbuf)):
            swapped = pltpu.roll(src, HEAD_DIM, axis=1)
            zero = jnp.zeros_like(src)
            buf[rows, 0 * LANES:1 * LANES] = jnp.where(low_o, src, zero).astype(BF16)
            buf[rows, 1 * LANES:2 * LANES] = jnp.where(low_o, zero, swapped).astype(BF16)
            buf[rows, 2 * LANES:3 * LANES] = jnp.where(low_o, swapped, zero).astype(BF16)
            buf[rows, 3 * LANES:4 * LANES] = jnp.where(low_o, zero, src).astype(BF16)

    qi = lax.broadcasted_iota(I32, (BLOCK, 2 * BLOCK), 0)
    kj = lax.broadcasted_iota(I32, (BLOCK, 2 * BLOCK), 1)
    band = (kj > qi) & (kj <= qi + BLOCK)
    band_first = band & ((kj >= BLOCK) | jnp.logical_not(first))

    def attend(n, q):
        rows = slice(n * BLOCK, (n + 2) * BLOCK)
        mask = band_first if n == 0 else band
        for hkv in range(N_KV_HEADS):
            k_bd = jnp.concatenate([kbuf[rows, (2 * hkv) * LANES:(2 * hkv + 1) * LANES],
                                    kbuf[rows, (2 * hkv + 1) * LANES:(2 * hkv + 2) * LANES]], axis=0)
            v_bd = jnp.concatenate([vbuf[rows, (2 * hkv) * LANES:(2 * hkv + 1) * LANES],
                                    vbuf[rows, (2 * hkv + 1) * LANES:(2 * hkv + 2) * LANES]], axis=0)
            for pair in range(2):
                hp = 2 * hkv + pair
                qp = q[:, hp * LANES:(hp + 1) * LANES]
                s = lax.dot_general(qp, k_bd, (((1,), (1,)), ((), ())),
                                    preferred_element_type=F32)
                ps, rden = [], []
                for hh in range(2):
                    sink = sink_ref[2 * hp + hh]
                    sh = jnp.where(mask, s[:, hh * 2 * BLOCK:(hh + 1) * 2 * BLOCK], -jnp.inf)
                    m = jnp.maximum(jnp.max(sh, axis=-1, keepdims=True), sink)
                    p = jnp.exp(sh - m)
                    den = jnp.sum(p, axis=-1, keepdims=True) + jnp.exp(sink - m)
                    ps.append(p.astype(BF16))
                    rden.append(1.0 / den)
                o = jnp.dot(jnp.concatenate(ps, axis=1), v_bd, preferred_element_type=F32)
                o = o * jnp.where(low_o, rden[0], rden[1])
                ybuf[n * BLOCK:(n + 1) * BLOCK, CONV_CH + hp * LANES:CONV_CH + (hp + 1) * LANES] = o.astype(BF16)

    x = x_ref[0]
    h = _rms(x, g_ref[...]).astype(BF16)
    ag = jnp.dot(h, win_ref[:, 0:COL_Q], preferred_element_type=F32) + bin_ref[:, 0:COL_Q]
    u = ag[:, :CONV_CH] * jax.nn.sigmoid(ag[:, CONV_CH:])
    store_u(CONV_HALO, u)
    halo[...] = u[ts - CONV_HALO:, :]
    q = jnp.dot(h, win_ref[:, COL_Q:COL_K], preferred_element_type=F32) + bin_ref[:, COL_Q:COL_K]
    q = (q * (HEAD_DIM ** -0.5)).astype(BF16)
    kv = jnp.dot(h, win_ref[:, COL_K:IN_COLS], preferred_element_type=F32) + bin_ref[:, COL_K:IN_COLS]

    for n in range(n_blocks):
        blk = slice(n * BLOCK, (n + 1) * BLOCK)
        store_kv(n, kv[blk, :])
        attend(n, q[blk, :])
        for r0 in range(n * BLOCK, (n + 1) * BLOCK, CONV_ROWS):
            conv_rows(r0)
        y = jnp.dot(ybuf[blk, :], wout_ref[...], preferred_element_type=F32)
        o_ref[0, blk, :] = x[blk, :] + y + bout_ref[...]


def _mixer(x, g, w_in, b_in, conv_w, conv_b, ln_g, ln_b, sinks, w_out, b_out, *, ts):
    B, S, D = x.shape
    row = lambda a: a.reshape(1, -1)
    const = lambda shape: pl.BlockSpec(shape, lambda b, s: (0,) * len(shape))
    return pl.pallas_call(
        functools.partial(_mixer_kernel, ts=ts),
        grid=(B, S // ts),
        in_specs=[
            pl.BlockSpec((1, ts, D), lambda b, s: (b, s, 0)),
            const((1, D)),
            const((D, IN_COLS)),
            const((1, IN_COLS)),
            const((CONV_KERNEL, CONV_CH)),
            const((1, CONV_CH)),
            const((1, CONV_CH)),
            const((1, CONV_CH)),
            pl.BlockSpec(memory_space=pltpu.SMEM),
            const((D, D)),
            const((1, D)),
        ],
        out_specs=pl.BlockSpec((1, ts, D), lambda b, s: (b, s, 0)),
        out_shape=jax.ShapeDtypeStruct((B, S, D), F32),
        scratch_shapes=[
            pltpu.VMEM((CONV_HALO, CONV_CH), F32),
            pltpu.VMEM((CONV_CH // LANES, 2 * (CONV_HALO + ts), LANES), F32),
            pltpu.VMEM((BLOCK + ts, 4 * LANES), BF16),
            pltpu.VMEM((BLOCK + ts, 4 * LANES), BF16),
            pltpu.VMEM((ts, D), BF16),
        ],
        compiler_params=pltpu.CompilerParams(
            dimension_semantics=("arbitrary", "arbitrary"), vmem_limit_bytes=VMEM_LIMIT),
        name="mixer",
    )(x, row(g), w_in.astype(BF16), row(b_in), conv_w, row(conv_b), row(ln_g), row(ln_b),
      sinks, w_out.astype(BF16), row(b_out))


def _ffn_kernel(x_ref, g_ref, wg_ref, wu_ref, wd_ref, o_ref):
    x = x_ref[...]
    h = _rms(x, g_ref[...]).astype(BF16)
    gate = jnp.dot(h, wg_ref[...], preferred_element_type=F32)
    up = jnp.dot(h, wu_ref[...], preferred_element_type=F32)
    a = (_silu(gate) * up).astype(BF16)
    o_ref[...] = x + jnp.dot(a, wd_ref[...], preferred_element_type=F32)


def _ffn(x, g, w_gate, w_up, w_down, *, tm):
    T, D = x.shape
    F = w_gate.shape[1]
    resident = lambda shape: pl.BlockSpec(shape, lambda i: (0, 0), pipeline_mode=pl.Buffered(1))
    return pl.pallas_call(
        _ffn_kernel,
        grid=(T // tm,),
        in_specs=[
            pl.BlockSpec((tm, D), lambda i: (i, 0)),
            pl.BlockSpec((1, D), lambda i: (0, 0)),
            resident((D, F)),
            resident((D, F)),
            resident((F, D)),
        ],
        out_specs=pl.BlockSpec((tm, D), lambda i: (i, 0)),
        out_shape=jax.ShapeDtypeStruct((T, D), F32),
        compiler_params=pltpu.CompilerParams(dimension_semantics=("arbitrary",), vmem_limit_bytes=VMEM_LIMIT),
        name="dense_ffn",
    )(x, g.reshape(1, D), w_gate.astype(BF16), w_up.astype(BF16), w_down.astype(BF16))


def _stage_store(stage_ref, val, accumulate=False):
    n = val.shape[0]
    for c in range(ROW_SUB):
        tile = val[:, c * LANES:(c + 1) * LANES].reshape(n // SUBLANES, SUBLANES, LANES)
        if accumulate:
            stage_ref[:, c * SUBLANES:(c + 1) * SUBLANES, :] += tile
        else:
            stage_ref[:, c * SUBLANES:(c + 1) * SUBLANES, :] = tile


def _stage_load(stage_ref):
    n = stage_ref.shape[0] * SUBLANES
    return jnp.concatenate(
        [stage_ref[:, c * SUBLANES:(c + 1) * SUBLANES, :].reshape(n, LANES) for c in range(ROW_SUB)], axis=1)


def _stage_to_rows(stage_ref, rows_ref):
    for g in range(stage_ref.shape[0]):
        for t in range(SUBLANES):
            rows_ref[g * SUBLANES + t] = stage_ref[g, pl.ds(t, ROW_SUB, stride=SUBLANES), :]


def _rows_to_stage(rows_ref, stage_ref):
    for g in range(stage_ref.shape[0]):
        for t in range(SUBLANES):
            stage_ref[g, pl.ds(t, ROW_SUB, stride=SUBLANES), :] = rows_ref[g * SUBLANES + t]


def _split_bf16(a):
    hi = a.astype(BF16)
    lo = (a - hi.astype(F32)).astype(BF16)
    return hi, lo


def _router_kernel(x_ref, g_ref, wr_hi_ref, wr_lo_ref, h3_ref, keys_ref, wts_ref, cnt_ref, stage, count, earlier):
    tm = x_ref.shape[0]
    lane = lax.broadcasted_iota(I32, (tm, LANES), 1)

    @pl.when(pl.program_id(0) == 0)
    def _():
        count[...] = jnp.zeros_like(count)
        r = lax.broadcasted_iota(I32, (tm, tm), 0)
        c = lax.broadcasted_iota(I32, (tm, tm), 1)
        earlier[...] = jnp.where(c < r, 1.0, 0.0).astype(BF16)

    hf = _rms(x_ref[...], g_ref[...])
    h_hi, h_lo = _split_bf16(hf)
    logits = (jnp.dot(h_hi, wr_hi_ref[...], preferred_element_type=F32)
              + jnp.dot(h_lo, wr_hi_ref[...], preferred_element_type=F32)
              + jnp.dot(h_hi, wr_lo_ref[...], preferred_element_type=F32))
    lg = jnp.where(lane < N_EXPERTS, logits, -jnp.inf)
    v1 = jnp.max(lg, axis=-1, keepdims=True)
    i1 = jnp.min(jnp.where(lg == v1, lane, LANES), axis=-1, keepdims=True)
    lg2 = jnp.where(lane == i1, -jnp.inf, lg)
    v2 = jnp.max(lg2, axis=-1, keepdims=True)
    i2 = jnp.min(jnp.where(lg2 == v2, lane, LANES), axis=-1, keepdims=True)
    t = jnp.exp(v2 - v1)
    w1 = 1.0 / (1.0 + t)
    w2 = t / (1.0 + t)

    onehot = jnp.where((lane == i1) | (lane == i2), 1.0, 0.0)
    before = count[...] + jnp.dot(earlier[...], onehot.astype(BF16), preferred_element_type=F32)
    rank1 = jnp.sum(jnp.where(lane == i1, before, 0.0), axis=-1, keepdims=True).astype(I32)
    rank2 = jnp.sum(jnp.where(lane == i2, before, 0.0), axis=-1, keepdims=True).astype(I32)
    count[...] += jnp.sum(onehot, axis=0, keepdims=True)

    key1 = i1 * (1 << KEY_SHIFT) + rank1
    key2 = i2 * (1 << KEY_SHIFT) + rank2
    keys_ref[...] = jnp.where(lane == 0, key1, jnp.where(lane == 1, key2, 0))
    wts_ref[...] = jnp.where(lane == 0, w1, jnp.where(lane == 1, w2, 0.0))
    cnt_ref[...] = count[...].astype(I32)
    _stage_store(stage, hf)
    _stage_to_rows(stage, h3_ref)


def _router(x, g, w_router, *, tm):
    T, D = x.shape
    wr = jnp.pad(w_router, ((0, 0), (0, LANES - N_EXPERTS)))
    wr_hi = wr.astype(BF16)
    wr_lo = (wr - wr_hi.astype(F32)).astype(BF16)
    return pl.pallas_call(
        _router_kernel,
        grid=(T // tm,),
        in_specs=[
            pl.BlockSpec((tm, D), lambda i: (i, 0)),
            pl.BlockSpec((1, D), lambda i: (0, 0)),
            pl.BlockSpec((D, LANES), lambda i: (0, 0)),
            pl.BlockSpec((D, LANES), lambda i: (0, 0)),
        ],
        out_specs=[
            pl.BlockSpec((tm, ROW_SUB, LANES), lambda i: (i, 0, 0)),
            pl.BlockSpec((tm, LANES), lambda i: (i, 0)),
            pl.BlockSpec((tm, LANES), lambda i: (i, 0)),
            pl.BlockSpec((1, LANES), lambda i: (0, 0)),
        ],
        out_shape=[
            jax.ShapeDtypeStruct((T, ROW_SUB, LANES), F32),
            jax.ShapeDtypeStruct((T, LANES), I32),
            jax.ShapeDtypeStruct((T, LANES), F32),
            jax.ShapeDtypeStruct((1, LANES), I32),
        ],
        scratch_shapes=[pltpu.VMEM((tm // SUBLANES, ROW_SUB * SUBLANES, LANES), F32),
                        pltpu.VMEM((1, LANES), F32),
                        pltpu.VMEM((tm, tm), BF16)],
        compiler_params=pltpu.CompilerParams(dimension_semantics=("arbitrary",), vmem_limit_bytes=VMEM_LIMIT),
        name="router",
    )(x, g.reshape(1, D), wr_hi, wr_lo)


def _dispatch_kernel(pos1_ref, pos2_ref, pad_start_ref, pad_len_ref, nu_ref, h3_ref, xs_ref,
                     zeros, sem, zsem, *, tmg):
    tm = h3_ref.shape[0]
    base = pl.program_id(0) * tm

    @pl.when(pl.program_id(0) == 0)
    def _():
        zeros[...] = jnp.zeros_like(zeros)

        def pad_copies(act):
            for e in range(N_EXPERTS):
                start, length = pad_start_ref[e], pad_len_ref[e]
                for b in range(tmg.bit_length() - 1):
                    size = 1 << b
                    offset = start + ((length >> (b + 1)) << (b + 1))
                    copy = pltpu.make_async_copy(zeros.at[pl.ds(0, size)], xs_ref.at[pl.ds(offset, size)], zsem)
                    pl.when(((length >> b) & 1) == 1)(lambda copy=copy: act(copy))

        def tile_copies(act):
            def body(i, carry):
                act(pltpu.make_async_copy(zeros, xs_ref.at[pl.ds(i * tmg, tmg)], zsem))
                return carry

            lax.fori_loop(nu_ref[0], xs_ref.shape[0] // tmg, body, 0)

        for act in (lambda c: c.start(), lambda c: c.wait()):
            pad_copies(act)
            tile_copies(act)

    def issue(blk, carry):
        for u in range(DMA_UNROLL):
            t = blk * DMA_UNROLL + u
            for k, pos_ref in enumerate((pos1_ref, pos2_ref)):
                pltpu.make_async_copy(h3_ref.at[t], xs_ref.at[pos_ref[base + t]], sem).start(priority=k)
        return carry

    lax.fori_loop(0, tm // DMA_UNROLL, issue, 0)
    for _ in range(2):
        pltpu.make_async_copy(h3_ref, xs_ref.at[pl.ds(0, tm)], sem).wait()


def _dispatch(pos1, pos2, pad_start, pad_len, n_used, h3, n_rows, *, tm, tmg):
    T = h3.shape[0]
    return pl.pallas_call(
        functools.partial(_dispatch_kernel, tmg=tmg),
        grid_spec=pltpu.PrefetchScalarGridSpec(
            num_scalar_prefetch=5,
            grid=(T // tm,),
            in_specs=[pl.BlockSpec((tm, ROW_SUB, LANES), lambda i, *_: (i, 0, 0))],
            out_specs=pl.BlockSpec(memory_space=pl.ANY),
            scratch_shapes=[pltpu.VMEM((tmg, ROW_SUB, LANES), F32),
                            pltpu.SemaphoreType.DMA, pltpu.SemaphoreType.DMA],
        ),
        out_shape=jax.ShapeDtypeStruct((n_rows, ROW_SUB, LANES), F32),
        compiler_params=pltpu.CompilerParams(dimension_semantics=("arbitrary",)),
        name="dispatch",
    )(pos1, pos2, pad_start, pad_len, n_used, h3)


def _expert_kernel(te_ref, nu_ref, x3_ref, wg_ref, wu_ref, wd_ref, y3_ref, stage, h_scr):
    del te_ref
    j = pl.program_id(1)
    used = pl.program_id(0) < nu_ref[0]

    @pl.when(jnp.logical_not(used) & (j == 0))
    def _():
        y3_ref[...] = jnp.zeros_like(y3_ref)

    @pl.when(used)
    def _():
        @pl.when(j == 0)
        def _():
            _rows_to_stage(x3_ref, stage)
            h_scr[...] = _stage_load(stage).astype(BF16)
            stage[...] = jnp.zeros_like(stage)

        h = h_scr[...]
        gate = jnp.dot(h, wg_ref[0], preferred_element_type=F32)
        up = jnp.dot(h, wu_ref[0], preferred_element_type=F32)
        act = (_silu(gate) * up).astype(BF16)
        _stage_store(stage, jnp.dot(act, wd_ref[0], preferred_element_type=F32), accumulate=True)

        @pl.when(j == pl.num_programs(1) - 1)
        def _():
            _stage_to_rows(stage, y3_ref)


def _experts(tile_end, n_used, xs3, w_gate, w_up, w_down, *, tmg, tf):
    n_rows = xs3.shape[0]
    E, D, F = w_gate.shape
    nj = F // tf
    row_blk = lambda i, j, te, nu: (jnp.minimum(i, nu[0] - 1), 0, 0)

    def expert(i, te, nu):
        ii = jnp.minimum(i, nu[0] - 1)
        return sum((ii >= te[e]).astype(I32) for e in range(E - 1))

    def chunk(i, j, nu):
        serp = lambda ii, jj: jnp.where(ii % 2 == 0, jj, nj - 1 - jj)
        return jnp.where(i < nu[0], serp(i, j), serp(nu[0] - 1, nj - 1))

    return pl.pallas_call(
        _expert_kernel,
        grid_spec=pltpu.PrefetchScalarGridSpec(
            num_scalar_prefetch=2,
            grid=(n_rows // tmg, nj),
            in_specs=[
                pl.BlockSpec((tmg, ROW_SUB, LANES), row_blk),
                pl.BlockSpec((1, D, tf), lambda i, j, te, nu: (expert(i, te, nu), 0, chunk(i, j, nu))),
                pl.BlockSpec((1, D, tf), lambda i, j, te, nu: (expert(i, te, nu), 0, chunk(i, j, nu))),
                pl.BlockSpec((1, tf, D), lambda i, j, te, nu: (expert(i, te, nu), chunk(i, j, nu), 0)),
            ],
            out_specs=pl.BlockSpec((tmg, ROW_SUB, LANES), lambda i, j, te, nu: (i, 0, 0)),
            scratch_shapes=[pltpu.VMEM((tmg // SUBLANES, ROW_SUB * SUBLANES, LANES), F32),
                            pltpu.VMEM((tmg, D), BF16)],
        ),
        out_shape=jax.ShapeDtypeStruct((n_rows, ROW_SUB, LANES), F32),
        compiler_params=pltpu.CompilerParams(
            dimension_semantics=("arbitrary", "arbitrary"), vmem_limit_bytes=VMEM_LIMIT),
        name="experts",
    )(tile_end, n_used, xs3, w_gate.astype(BF16), w_up.astype(BF16), w_down.astype(BF16))


def _combine_kernel(pos1_ref, pos2_ref, x_ref, wts_ref, fg_ref, y3_ref, o_ref, rows, stage, sems):
    tm = x_ref.shape[0]
    i = pl.program_id(0)

    def issue(tile, slot):
        base = tile * tm

        def body(blk, carry):
            for u in range(DMA_UNROLL):
                t = blk * DMA_UNROLL + u
                for k, pos_ref in enumerate((pos1_ref, pos2_ref)):
                    pltpu.make_async_copy(y3_ref.at[pos_ref[base + t]], rows.at[slot, k, t],
                                          sems.at[slot]).start(priority=k)
            return carry

        lax.fori_loop(0, tm // DMA_UNROLL, body, 0)

    @pl.when(i == 0)
    def _():
        issue(0, 0)

    @pl.when(i + 1 < pl.num_programs(0))
    def _():
        issue(i + 1, (i + 1) % 2)

    slot = i % 2
    for k in range(2):
        pltpu.make_async_copy(y3_ref.at[pl.ds(0, tm)], rows.at[slot, k], sems.at[slot]).wait()

    w = wts_ref[...]
    _rows_to_stage(rows.at[slot, 0], stage)
    y1 = _stage_load(stage) * w[:, 0:1]
    _rows_to_stage(rows.at[slot, 1], stage)
    y2 = _stage_load(stage) * w[:, 1:2]
    o_ref[...] = _rms(x_ref[...] + (y1 + y2), fg_ref[...])


def _combine(pos1, pos2, x, wts, final_g, y3, *, tm):
    T, D = x.shape
    return pl.pallas_call(
        _combine_kernel,
        grid_spec=pltpu.PrefetchScalarGridSpec(
            num_scalar_prefetch=2,
            grid=(T // tm,),
            in_specs=[
                pl.BlockSpec((tm, D), lambda i, *_: (i, 0)),
                pl.BlockSpec((tm, LANES), lambda i, *_: (i, 0)),
                pl.BlockSpec((1, D), lambda i, *_: (0, 0)),
                pl.BlockSpec(memory_space=pl.ANY),
            ],
            out_specs=pl.BlockSpec((tm, D), lambda i, *_: (i, 0)),
            scratch_shapes=[pltpu.VMEM((2, 2, tm, ROW_SUB, LANES), F32),
                            pltpu.VMEM((tm // SUBLANES, ROW_SUB * SUBLANES, LANES), F32),
                            pltpu.SemaphoreType.DMA((2,))],
        ),
        out_shape=jax.ShapeDtypeStruct((T, D), F32),
        compiler_params=pltpu.CompilerParams(dimension_semantics=("arbitrary",), vmem_limit_bytes=VMEM_LIMIT),
        name="combine",
    )(pos1, pos2, x, wts, final_g.reshape(1, D), y3)


def _moe(x, g, w_router, w_gate, w_up, w_down, final_g, *, tm, tmg, tf):
    T, D = x.shape
    E = N_EXPERTS
    h3, keys, wts, cnt = _router(x, g, w_router, tm=tm)
    key1, key2 = keys[:, 0], keys[:, 1]

    counts = cnt[0, :E]
    tiles = (counts + (tmg - 1)) // tmg
    tile_end = jnp.cumsum(tiles).astype(I32)
    starts = ((tile_end - tiles) * tmg).astype(I32)
    n_tiles = 2 * T // tmg + E
    n_used = tile_end[-1:]
    pad_start = (starts + counts).astype(I32)
    pad_len = (tiles * tmg - counts).astype(I32)
    pos1 = starts[key1 >> KEY_SHIFT] + (key1 & KEY_MASK)
    pos2 = starts[key2 >> KEY_SHIFT] + (key2 & KEY_MASK)

    xs3 = _dispatch(pos1, pos2, pad_start, pad_len, n_used, h3, n_tiles * tmg, tm=tm, tmg=tmg)
    y3 = _experts(tile_end, n_used, xs3, w_gate, w_up, w_down, tmg=tmg, tf=tf)
    return _combine(pos1, pos2, x, wts, final_g, y3, tm=tm)


def kernel(x, attn_norm, ffn_norm, w_in, b_in, conv_w, conv_b, conv_ln_g, conv_ln_b, sinks, w_out, b_out,
           ffn_w_gate, ffn_w_up, ffn_w_down, moe_router, moe_w_gate, moe_w_up, moe_w_down, final_norm):
    B, S, D = x.shape
    ts = min(S, 512)
    tm = min(B * S, 512)

    def mixer(x, l):
        return _mixer(x, attn_norm[l], w_in[l], b_in[l], conv_w[l], conv_b[l], conv_ln_g[l], conv_ln_b[l],
                      sinks[l], w_out[l], b_out[l], ts=ts)

    x = mixer(x, 0)
    x = _ffn(x.reshape(B * S, D), ffn_norm[0], ffn_w_gate[0], ffn_w_up[0], ffn_w_down[0],
             tm=tm)
    x = mixer(x.reshape(B, S, D), 1)
    x = _moe(x.reshape(B * S, D), ffn_norm[1], moe_router[0], moe_w_gate[0], moe_w_up[0], moe_w_down[0],
             final_norm, tm=tm, tmg=tm, tf=moe_w_gate.shape[3] // 2)
    return x.reshape(B, S, D)
```

```python
import functools

import jax
import jax.numpy as jnp
from jax import lax
from jax.experimental import pallas as pl
from jax.experimental.pallas import tpu as pltpu

F32 = jnp.float32
BF16 = jnp.bfloat16
I32 = jnp.int32

D_MODEL = 1024
CONV_CH = 512
CONV_KERNEL = 31
HEAD_DIM = 64
N_Q_HEADS = 8
N_KV_HEADS = 2
ATTN_WIDTH = N_Q_HEADS * HEAD_DIM
KV_WIDTH = N_KV_HEADS * HEAD_DIM
BLOCK = 128
N_EXPERTS = 8
EPS = 1e-5

LANES = 128
SUBLANES = 8
CONV_HALO = 32
CONV_ROWS = 32
VMEM_LIMIT = 56 * 1024 * 1024

COL_Q = 2 * CONV_CH
COL_K = COL_Q + ATTN_WIDTH
COL_V = COL_K + KV_WIDTH
IN_COLS = COL_V + KV_WIDTH

ROW_SUB = D_MODEL // LANES
KEY_SHIFT = 20
KEY_MASK = (1 << KEY_SHIFT) - 1
DMA_UNROLL = 8


def _rms(x, g):
    ms = jnp.mean(x * x, axis=-1, keepdims=True)
    return x * lax.rsqrt(ms + EPS) * g


def _silu(x):
    return x * jax.nn.sigmoid(x)


def _mixer_kernel(x_ref, g_ref, win_ref, bin_ref, cw_ref, cb_ref, lng_ref, lnb_ref,
                  sink_ref, wout_ref, bout_ref, o_ref, halo, sbuf, kbuf, vbuf, ybuf, *, ts):
    first = pl.program_id(1) == 0
    n_chunks = CONV_CH // LANES
    n_blocks = ts // BLOCK

    def store_u(row0, val):
        for c in range(n_chunks):
            sbuf[c, pl.ds(2 * row0, val.shape[0], stride=2), :] = val[:, c * LANES:(c + 1) * LANES]

    store_u(0, jnp.where(first, 0.0, halo[...]))
    tap0 = CONV_HALO - (CONV_KERNEL - 1)

    def conv_rows(r0):
        accs = [jnp.broadcast_to(cb_ref[:, c * LANES:(c + 1) * LANES], (CONV_ROWS, LANES)) for c in range(n_chunks)]
        for j in range(CONV_KERNEL):
            for c in range(n_chunks):
                accs[c] = accs[c] + (cw_ref[j:j + 1, c * LANES:(c + 1) * LANES]
                                     * sbuf[c, pl.ds(2 * (r0 + tap0 + j), CONV_ROWS, stride=2), :])
        acc = jnp.concatenate(accs, axis=1)
        mu = jnp.mean(acc, axis=-1, keepdims=True)
        xc = acc - mu
        var = jnp.mean(xc * xc, axis=-1, keepdims=True)
        yn = xc * lax.rsqrt(var + EPS) * lng_ref[...] + lnb_ref[...]
        ybuf[r0:r0 + CONV_ROWS, 0:CONV_CH] = _silu(yn).astype(BF16)

    for buf in (kbuf, vbuf):
        buf[0:BLOCK, :] = jnp.where(first, jnp.zeros((BLOCK, 4 * LANES), BF16), buf[ts:ts + BLOCK, :])

    low_o = lax.broadcasted_iota(I32, (BLOCK, LANES), 1) < HEAD_DIM

    def store_kv(n, kv):
        rows = slice((n + 1) * BLOCK, (n + 2) * BLOCK)
        for src, buf in ((kv[:, 0:LANES], kbuf), (kv[:, LANES:2 * LANES], vbuf)):
            swapped = pltpu.roll(src, HEAD_DIM, axis=1)
            zero = jnp.zeros_like(src)
            buf[rows, 0 * LANES:1 * LANES] = jnp.where(low_o, src, zero).astype(BF16)
            buf[rows, 1 * LANES:2 * LANES] = jnp.where(low_o, zero, swapped).astype(BF16)
            buf[rows, 2 * LANES:3 * LANES] = jnp.where(low_o, swapped, zero).astype(BF16)
            buf[rows, 3 * LANES:4 * LANES] = jnp.where(low_o, zero, src).astype(BF16)

    qi = lax.broadcasted_iota(I32, (BLOCK, 2 * BLOCK), 0)
    kj = lax.broadcasted_iota(I32, (BLOCK, 2 * BLOCK), 1)
    band = (kj > qi) & (kj <= qi + BLOCK)
    band_first = band & ((kj >= BLOCK) | jnp.logical_not(first))

    def attend(n, q):
        rows = slice(n * BLOCK, (n + 2) * BLOCK)
        mask = band_first if n == 0 else band
        for hkv in range(N_KV_HEADS):
            k_bd = jnp.concatenate([kbuf[rows, (2 * hkv) * LANES:(2 * hkv + 1) * LANES],
                                    kbuf[rows, (2 * hkv + 1) * LANES:(2 * hkv + 2) * LANES]], axis=0)
            v_bd = jnp.concatenate([vbuf[rows, (2 * hkv) * LANES:(2 * hkv + 1) * LANES],
                                    vbuf[rows, (2 * hkv + 1) * LANES:(2 * hkv + 2) * LANES]], axis=0)
            for pair in range(2):
                hp = 2 * hkv + pair
                qp = q[:, hp * LANES:(hp + 1) * LANES]
                s = lax.dot_general(qp, k_bd, (((1,), (1,)), ((), ())),
                                    preferred_element_type=F32)
                ps, rden = [], []
                for hh in range(2):
                    sink = sink_ref[2 * hp + hh]
                    sh = jnp.where(mask, s[:, hh * 2 * BLOCK:(hh + 1) * 2 * BLOCK], -jnp.inf)
                    m = jnp.maximum(jnp.max(sh, axis=-1, keepdims=True), sink)
                    p = jnp.exp(sh - m)
                    den = jnp.sum(p, axis=-1, keepdims=True) + jnp.exp(sink - m)
                    ps.append(p.astype(BF16))
                    rden.append(1.0 / den)
                o = jnp.dot(jnp.concatenate(ps, axis=1), v_bd, preferred_element_type=F32)
                o = o * jnp.where(low_o, rden[0], rden[1])
                ybuf[n * BLOCK:(n + 1) * BLOCK, CONV_CH + hp * LANES:CONV_CH + (hp + 1) * LANES] = o.astype(BF16)

    x = x_ref[0]
    h = _rms(x, g_ref[...]).astype(BF16)
    ag = jnp.dot(h, win_ref[:, 0:COL_Q], preferred_element_type=F32) + bin_ref[:, 0:COL_Q]
    u = ag[:, :CONV_CH] * jax.nn.sigmoid(ag[:, CONV_CH:])
    store_u(CONV_HALO, u)
    halo[...] = u[ts - CONV_HALO:, :]
    q = jnp.dot(h, win_ref[:, COL_Q:COL_K], preferred_element_type=F32) + bin_ref[:, COL_Q:COL_K]
    q = (q * (HEAD_DIM ** -0.5)).astype(BF16)
    kv = jnp.dot(h, win_ref[:, COL_K:IN_COLS], preferred_element_type=F32) + bin_ref[:, COL_K:IN_COLS]

    for n in range(n_blocks):
        blk = slice(n * BLOCK, (n + 1) * BLOCK)
        store_kv(n, kv[blk, :])
        attend(n, q[blk, :])
        for r0 in range(n * BLOCK, (n + 1) * BLOCK, CONV_ROWS):
            conv_rows(r0)
        y = jnp.dot(ybuf[blk, :], wout_ref[...], preferred_element_type=F32)
        o_ref[0, blk, :] = x[blk, :] + y + bout_ref[...]


def _mixer(x, g, w_in, b_in, conv_w, conv_b, ln_g, ln_b, sinks, w_out, b_out, *, ts):
    B, S, D = x.shape
    row = lambda a: a.reshape(1, -1)
    const = lambda shape: pl.BlockSpec(shape, lambda b, s: (0,) * len(shape))
    return pl.pallas_call(
        functools.partial(_mixer_kernel, ts=ts),
        grid=(B, S // ts),
        in_specs=[
            pl.BlockSpec((1, ts, D), lambda b, s: (b, s, 0)),
            const((1, D)),
            const((D, IN_COLS)),
            const((1, IN_COLS)),
            const((CONV_KERNEL, CONV_CH)),
            const((1, CONV_CH)),
            const((1, CONV_CH)),
            const((1, CONV_CH)),
            pl.BlockSpec(memory_space=pltpu.SMEM),
            const((D, D)),
            const((1, D)),
        ],
        out_specs=pl.BlockSpec((1, ts, D), lambda b, s: (b, s, 0)),
        out_shape=jax.ShapeDtypeStruct((B, S, D), F32),
        scratch_shapes=[
            pltpu.VMEM((CONV_HALO, CONV_CH), F32),
            pltpu.VMEM((CONV_CH // LANES, 2 * (CONV_HALO + ts), LANES), F32),
            pltpu.VMEM((BLOCK + ts, 4 * LANES), BF16),
            pltpu.VMEM((BLOCK + ts, 4 * LANES), BF16),
            pltpu.VMEM((ts, D), BF16),
        ],
        compiler_params=pltpu.CompilerParams(
            dimension_semantics=("arbitrary", "arbitrary"), vmem_limit_bytes=VMEM_LIMIT),
        name="mixer",
    )(x, row(g), w_in.astype(BF16), row(b_in), conv_w, row(conv_b), row(ln_g), row(ln_b),
      sinks, w_out.astype(BF16), row(b_out))


def _ffn_kernel(x_ref, g_ref, wg_ref, wu_ref, wd_ref, o_ref):
    x = x_ref[...]
    h = _rms(x, g_ref[...]).astype(BF16)
    gate = jnp.dot(h, wg_ref[...], preferred_element_type=F32)
    up = jnp.dot(h, wu_ref[...], preferred_element_type=F32)
    a = (_silu(gate) * up).astype(BF16)
    o_ref[...] = x + jnp.dot(a, wd_ref[...], preferred_element_type=F32)


def _ffn(x, g, w_gate, w_up, w_down, *, tm):
    T, D = x.shape
    F = w_gate.shape[1]
    resident = lambda shape: pl.BlockSpec(shape, lambda i: (0, 0), pipeline_mode=pl.Buffered(1))
    return pl.pallas_call(
        _ffn_kernel,
        grid=(T // tm,),
        in_specs=[
            pl.BlockSpec((tm, D), lambda i: (i, 0)),
            pl.BlockSpec((1, D), lambda i: (0, 0)),
            resident((D, F)),
            resident((D, F)),
            resident((F, D)),
        ],
        out_specs=pl.BlockSpec((tm, D), lambda i: (i, 0)),
        out_shape=jax.ShapeDtypeStruct((T, D), F32),
        compiler_params=pltpu.CompilerParams(dimension_semantics=("arbitrary",), vmem_limit_bytes=VMEM_LIMIT),
        name="dense_ffn",
    )(x, g.reshape(1, D), w_gate.astype(BF16), w_up.astype(BF16), w_down.astype(BF16))


def _stage_store(stage_ref, val, accumulate=False):
    n = val.shape[0]
    for c in range(ROW_SUB):
        tile = val[:, c * LANES:(c + 1) * LANES].reshape(n // SUBLANES, SUBLANES, LANES)
        if accumulate:
            stage_ref[:, c * SUBLANES:(c + 1) * SUBLANES, :] += tile
        else:
            stage_ref[:, c * SUBLANES:(c + 1) * SUBLANES, :] = tile


def _stage_load(stage_ref):
    n = stage_ref.shape[0] * SUBLANES
    return jnp.concatenate(
        [stage_ref[:, c * SUBLANES:(c + 1) * SUBLANES, :].reshape(n, LANES) for c in range(ROW_SUB)], axis=1)


def _stage_to_rows(stage_ref, rows_ref):
    for g in range(stage_ref.shape[0]):
        for t in range(SUBLANES):
            rows_ref[g * SUBLANES + t] = stage_ref[g, pl.ds(t, ROW_SUB, stride=SUBLANES), :]


def _rows_to_stage(rows_ref, stage_ref):
    for g in range(stage_ref.shape[0]):
        for t in range(SUBLANES):
            stage_ref[g, pl.ds(t, ROW_SUB, stride=SUBLANES), :] = rows_ref[g * SUBLANES + t]


def _split_bf16(a):
    hi = a.astype(BF16)
    lo = (a - hi.astype(F32)).astype(BF16)
    return hi, lo


def _router_kernel(x_ref, g_ref, wr_cat_ref, h3_ref, keys_ref, wts_ref, cnt_ref, stage, count, earlier):
    tm = x_ref.shape[0]
    lane = lax.broadcasted_iota(I32, (tm, LANES), 1)

    @pl.when(pl.program_id(0) == 0)
    def _():
        count[...] = jnp.zeros_like(count)
        r = lax.broadcasted_iota(I32, (tm, tm), 0)
        c = lax.broadcasted_iota(I32, (tm, tm), 1)
        earlier[...] = jnp.where(c < r, 1.0, 0.0).astype(BF16)

    hf = _rms(x_ref[...], g_ref[...])
    h_hi, h_lo = _split_bf16(hf)
    hi_terms = jnp.dot(h_hi, wr_cat_ref[...], preferred_element_type=F32)
    logits = (hi_terms[:, :LANES] + hi_terms[:, LANES:]
              + jnp.dot(h_lo, wr_cat_ref[:, :LANES], preferred_element_type=F32))
    lg = jnp.where(lane < N_EXPERTS, logits, -jnp.inf)
    v1 = jnp.max(lg, axis=-1, keepdims=True)
    i1 = jnp.min(jnp.where(lg == v1, lane, LANES), axis=-1, keepdims=True)
    lg2 = jnp.where(lane == i1, -jnp.inf, lg)
    v2 = jnp.max(lg2, axis=-1, keepdims=True)
    i2 = jnp.min(jnp.where(lg2 == v2, lane, LANES), axis=-1, keepdims=True)
    t = jnp.exp(v2 - v1)
    w1 = 1.0 / (1.0 + t)
    w2 = t / (1.0 + t)

    onehot = jnp.where((lane == i1) | (lane == i2), 1.0, 0.0)
    before = count[...] + jnp.dot(earlier[...], onehot.astype(BF16), preferred_element_type=F32)
    rank1 = jnp.sum(jnp.where(lane == i1, before, 0.0), axis=-1, keepdims=True).astype(I32)
    rank2 = jnp.sum(jnp.where(lane == i2, before, 0.0), axis=-1, keepdims=True).astype(I32)
    count[...] += jnp.sum(onehot, axis=0, keepdims=True)

    key1 = i1 * (1 << KEY_SHIFT) + rank1
    key2 = i2 * (1 << KEY_SHIFT) + rank2
    keys_ref[...] = jnp.where(lane == 0, key1, jnp.where(lane == 1, key2, 0))
    wts_ref[...] = jnp.where(lane == 0, w1, jnp.where(lane == 1, w2, 0.0))
    cnt_ref[...] = count[...].astype(I32)
    _stage_store(stage, hf)
    _stage_to_rows(stage, h3_ref)


def _router(x, g, w_router, *, tm):
    T, D = x.shape
    wr = jnp.pad(w_router, ((0, 0), (0, LANES - N_EXPERTS)))
    wr_hi = wr.astype(BF16)
    wr_lo = (wr - wr_hi.astype(F32)).astype(BF16)
    return pl.pallas_call(
        _router_kernel,
        grid=(T // tm,),
        in_specs=[
            pl.BlockSpec((tm, D), lambda i: (i, 0)),
            pl.BlockSpec((1, D), lambda i: (0, 0)),
            pl.BlockSpec((D, 2 * LANES), lambda i: (0, 0)),
        ],
        out_specs=[
            pl.BlockSpec((tm, ROW_SUB, LANES), lambda i: (i, 0, 0)),
            pl.BlockSpec((tm, LANES), lambda i: (i, 0)),
            pl.BlockSpec((tm, LANES), lambda i: (i, 0)),
            pl.BlockSpec((1, LANES), lambda i: (0, 0)),
        ],
        out_shape=[
            jax.ShapeDtypeStruct((T, ROW_SUB, LANES), F32),
            jax.ShapeDtypeStruct((T, LANES), I32),
            jax.ShapeDtypeStruct((T, LANES), F32),
            jax.ShapeDtypeStruct((1, LANES), I32),
        ],
        scratch_shapes=[pltpu.VMEM((tm // SUBLANES, ROW_SUB * SUBLANES, LANES), F32),
                        pltpu.VMEM((1, LANES), F32),
                        pltpu.VMEM((tm, tm), BF16)],
        compiler_params=pltpu.CompilerParams(dimension_semantics=("arbitrary",), vmem_limit_bytes=VMEM_LIMIT),
        name="router",
    )(x, g.reshape(1, D), jnp.concatenate([wr_hi, wr_lo], axis=1))


def _dispatch_kernel(pos1_ref, pos2_ref, pad_start_ref, pad_len_ref, nu_ref, h3_ref, xs_ref,
                     zeros, sem, zsem, *, tmg):
    tm = h3_ref.shape[0]
    base = pl.program_id(0) * tm

    @pl.when(pl.program_id(0) == 0)
    def _():
        zeros[...] = jnp.zeros_like(zeros)

        def pad_copies(act):
            for e in range(N_EXPERTS):
                start, length = pad_start_ref[e], pad_len_ref[e]
                for b in range(tmg.bit_length() - 1):
                    size = 1 << b
                    offset = start + ((length >> (b + 1)) << (b + 1))
                    copy = pltpu.make_async_copy(zeros.at[pl.ds(0, size)], xs_ref.at[pl.ds(offset, size)], zsem)
                    pl.when(((length >> b) & 1) == 1)(lambda copy=copy: act(copy))

        def tile_copies(act):
            def body(i, carry):
                act(pltpu.make_async_copy(zeros, xs_ref.at[pl.ds(i * tmg, tmg)], zsem))
                return carry

            lax.fori_loop(nu_ref[0], xs_ref.shape[0] // tmg, body, 0)

        for act in (lambda c: c.start(), lambda c: c.wait()):
            pad_copies(act)
            tile_copies(act)

    def issue(blk, carry):
        for u in range(DMA_UNROLL):
            t = blk * DMA_UNROLL + u
            for k, pos_ref in enumerate((pos1_ref, pos2_ref)):
                pltpu.make_async_copy(h3_ref.at[t], xs_ref.at[pos_ref[base + t]], sem).start(priority=k)
        return carry

    lax.fori_loop(0, tm // DMA_UNROLL, issue, 0)
    for _ in range(2):
        pltpu.make_async_copy(h3_ref, xs_ref.at[pl.ds(0, tm)], sem).wait()


def _dispatch(pos1, pos2, pad_start, pad_len, n_used, h3, n_rows, *, tm, tmg):
    T = h3.shape[0]
    return pl.pallas_call(
        functools.partial(_dispatch_kernel, tmg=tmg),
        grid_spec=pltpu.PrefetchScalarGridSpec(
            num_scalar_prefetch=5,
            grid=(T // tm,),
            in_specs=[pl.BlockSpec((tm, ROW_SUB, LANES), lambda i, *_: (i, 0, 0))],
            out_specs=pl.BlockSpec(memory_space=pl.ANY),
            scratch_shapes=[pltpu.VMEM((tmg, ROW_SUB, LANES), F32),
                            pltpu.SemaphoreType.DMA, pltpu.SemaphoreType.DMA],
        ),
        out_shape=jax.ShapeDtypeStruct((n_rows, ROW_SUB, LANES), F32),
        compiler_params=pltpu.CompilerParams(dimension_semantics=("arbitrary",)),
        name="dispatch",
    )(pos1, pos2, pad_start, pad_len, n_used, h3)


def _expert_kernel(te_ref, nu_ref, x3_ref, wg_ref, wu_ref, wd_ref, y3_ref, stage, h_scr):
    del te_ref
    j = pl.program_id(1)
    used = pl.program_id(0) < nu_ref[0]

    @pl.when(jnp.logical_not(used) & (j == 0))
    def _():
        y3_ref[...] = jnp.zeros_like(y3_ref)

    @pl.when(used)
    def _():
        @pl.when(j == 0)
        def _():
            _rows_to_stage(x3_ref, stage)
            h_scr[...] = _stage_load(stage).astype(BF16)
            stage[...] = jnp.zeros_like(stage)

        h = h_scr[...]
        gate = jnp.dot(h, wg_ref[0], preferred_element_type=F32)
        up = jnp.dot(h, wu_ref[0], preferred_element_type=F32)
        act = (_silu(gate) * up).astype(BF16)
        _stage_store(stage, jnp.dot(act, wd_ref[0], preferred_element_type=F32), accumulate=True)

        @pl.when(j == pl.num_programs(1) - 1)
        def _():
            _stage_to_rows(stage, y3_ref)


def _experts(tile_end, n_used, xs3, w_gate, w_up, w_down, *, tmg, tf):
    n_rows = xs3.shape[0]
    E, D, F = w_gate.shape
    nj = F // tf
    row_blk = lambda i, j, te, nu: (jnp.minimum(i, nu[0] - 1), 0, 0)

    def expert(i, te, nu):
        ii = jnp.minimum(i, nu[0] - 1)
        return sum((ii >= te[e]).astype(I32) for e in range(E - 1))

    def chunk(i, j, nu):
        serp = lambda ii, jj: jnp.where(ii % 2 == 0, jj, nj - 1 - jj)
        return jnp.where(i < nu[0], serp(i, j), serp(nu[0] - 1, nj - 1))

    return pl.pallas_call(
        _expert_kernel,
        grid_spec=pltpu.PrefetchScalarGridSpec(
            num_scalar_prefetch=2,
            grid=(n_rows // tmg, nj),
            in_specs=[
                pl.BlockSpec((tmg, ROW_SUB, LANES), row_blk),
                pl.BlockSpec((1, D, tf), lambda i, j, te, nu: (expert(i, te, nu), 0, chunk(i, j, nu))),
                pl.BlockSpec((1, D, tf), lambda i, j, te, nu: (expert(i, te, nu), 0, chunk(i, j, nu))),
                pl.BlockSpec((1, tf, D), lambda i, j, te, nu: (expert(i, te, nu), chunk(i, j, nu), 0)),
            ],
            out_specs=pl.BlockSpec((tmg, ROW_SUB, LANES), lambda i, j, te, nu: (i, 0, 0)),
            scratch_shapes=[pltpu.VMEM((tmg // SUBLANES, ROW_SUB * SUBLANES, LANES), F32),
                            pltpu.VMEM((tmg, D), BF16)],
        ),
        out_shape=jax.ShapeDtypeStruct((n_rows, ROW_SUB, LANES), F32),
        compiler_params=pltpu.CompilerParams(
            dimension_semantics=("arbitrary", "arbitrary"), vmem_limit_bytes=VMEM_LIMIT),
        name="experts",
    )(tile_end, n_used, xs3, w_gate.astype(BF16), w_up.astype(BF16), w_down.astype(BF16))


def _combine_kernel(pos1_ref, pos2_ref, x_ref, wts_ref, fg_ref, y3_ref, o_ref, rows, stage, sems):
    tm = x_ref.shape[0]
    i = pl.program_id(0)

    def issue(tile, slot):
        base = tile * tm

        def body(blk, carry):
            for u in range(DMA_UNROLL):
                t = blk * DMA_UNROLL + u
                for k, pos_ref in enumerate((pos1_ref, pos2_ref)):
                    pltpu.make_async_copy(y3_ref.at[pos_ref[base + t]], rows.at[slot, k, t],
                                          sems.at[slot]).start(priority=k)
            return carry

        lax.fori_loop(0, tm // DMA_UNROLL, body, 0)

    @pl.when(i == 0)
    def _():
        issue(0, 0)

    @pl.when(i + 1 < pl.num_programs(0))
    def _():
        issue(i + 1, (i + 1) % 2)

    slot = i % 2
    for k in range(2):
        pltpu.make_async_copy(y3_ref.at[pl.ds(0, tm)], rows.at[slot, k], sems.at[slot]).wait()

    w = wts_ref[...]
    _rows_to_stage(rows.at[slot, 0], stage)
    y1 = _stage_load(stage) * w[:, 0:1]
    _rows_to_stage(rows.at[slot, 1], stage)
    y2 = _stage_load(stage) * w[:, 1:2]
    o_ref[...] = _rms(x_ref[...] + (y1 + y2), fg_ref[...])


def _combine(pos1, pos2, x, wts, final_g, y3, *, tm):
    T, D = x.shape
    return pl.pallas_call(
        _combine_kernel,
        grid_spec=pltpu.PrefetchScalarGridSpec(
            num_scalar_prefetch=2,
            grid=(T // tm,),
            in_specs=[
                pl.BlockSpec((tm, D), lambda i, *_: (i, 0)),
                pl.BlockSpec((tm, LANES), lambda i, *_: (i, 0)),
                pl.BlockSpec((1, D), lambda i, *_: (0, 0)),
                pl.BlockSpec(memory_space=pl.ANY),
            ],
            out_specs=pl.BlockSpec((tm, D), lambda i, *_: (i, 0)),
            scratch_shapes=[pltpu.VMEM((2, 2, tm, ROW_SUB, LANES), F32),
                            pltpu.VMEM((tm // SUBLANES, ROW_SUB * SUBLANES, LANES), F32),
                            pltpu.SemaphoreType.DMA((2,))],
        ),
        out_shape=jax.ShapeDtypeStruct((T, D), F32),
        compiler_params=pltpu.CompilerParams(dimension_semantics=("arbitrary",), vmem_limit_bytes=VMEM_LIMIT),
        name="combine",
    )(pos1, pos2, x, wts, final_g.reshape(1, D), y3)


def _moe(x, g, w_router, w_gate, w_up, w_down, final_g, *, tm, tmg, tf):
    T, D = x.shape
    E = N_EXPERTS
    h3, keys, wts, cnt = _router(x, g, w_router, tm=tm)
    key1, key2 = keys[:, 0], keys[:, 1]

    counts = cnt[0, :E]
    tiles = (counts + (tmg - 1)) // tmg
    tile_end = jnp.cumsum(tiles).astype(I32)
    starts = ((tile_end - tiles) * tmg).astype(I32)
    n_tiles = 2 * T // tmg + E
    n_used = tile_end[-1:]
    pad_start = (starts + counts).astype(I32)
    pad_len = (tiles * tmg - counts).astype(I32)
    def position(key):
        pos = key & KEY_MASK
        for e in range(E):
            pos = pos + jnp.where((key >> KEY_SHIFT) == e, starts[e], 0)
        return pos

    pos1, pos2 = position(key1), position(key2)

    xs3 = _dispatch(pos1, pos2, pad_start, pad_len, n_used, h3, n_tiles * tmg, tm=tm, tmg=tmg)
    y3 = _experts(tile_end, n_used, xs3, w_gate, w_up, w_down, tmg=tmg, tf=tf)
    return _combine(pos1, pos2, x, wts, final_g, y3, tm=tm)


def kernel(x, attn_norm, ffn_norm, w_in, b_in, conv_w, conv_b, conv_ln_g, conv_ln_b, sinks, w_out, b_out,
           ffn_w_gate, ffn_w_up, ffn_w_down, moe_router, moe_w_gate, moe_w_up, moe_w_down, final_norm):
    B, S, D = x.shape
    ts = min(S, 512)
    tm = min(B * S, 512)

    def mixer(x, l):
        return _mixer(x, attn_norm[l], w_in[l], b_in[l], conv_w[l], conv_b[l], conv_ln_g[l], conv_ln_b[l],
                      sinks[l], w_out[l], b_out[l], ts=ts)

    x = mixer(x, 0)
    x = _ffn(x.reshape(B * S, D), ffn_norm[0], ffn_w_gate[0], ffn_w_up[0], ffn_w_down[0],
             tm=tm)
    x = mixer(x.reshape(B, S, D), 1)
    x = _moe(x.reshape(B * S, D), ffn_norm[1], moe_router[0], moe_w_gate[0], moe_w_up[0], moe_w_down[0],
             final_norm, tm=tm, tmg=tm, tf=moe_w_gate.shape[3] // 2)
    return x.reshape(B, S, D)
```

```python
import functools

import jax
import jax.numpy as jnp
from jax import lax
from jax.experimental import pallas as pl
from jax.experimental.pallas import tpu as pltpu

F32 = jnp.float32
BF16 = jnp.bfloat16
I32 = jnp.int32

D_MODEL = 1024
CONV_CH = 512
CONV_KERNEL = 31
HEAD_DIM = 64
N_Q_HEADS = 8
N_KV_HEADS = 2
ATTN_WIDTH = N_Q_HEADS * HEAD_DIM
KV_WIDTH = N_KV_HEADS * HEAD_DIM
BLOCK = 128
N_EXPERTS = 8
EPS = 1e-5

LANES = 128
SUBLANES = 8
CONV_HALO = 32
CONV_ROWS = 32
VMEM_LIMIT = 56 * 1024 * 1024

COL_Q = 2 * CONV_CH
COL_K = COL_Q + ATTN_WIDTH
COL_V = COL_K + KV_WIDTH
IN_COLS = COL_V + KV_WIDTH

ROW_SUB = D_MODEL // LANES
KEY_SHIFT = 20
KEY_MASK = (1 << KEY_SHIFT) - 1
DMA_UNROLL = 8


def _rms(x, g):
    ms = jnp.mean(x * x, axis=-1, keepdims=True)
    return x * lax.rsqrt(ms + EPS) * g


def _silu(x):
    return x * jax.nn.sigmoid(x)


def _mixer_kernel(x_ref, g_ref, win_ref, bin_ref, cw_ref, cb_ref, lng_ref, lnb_ref,
                  sink_ref, wout_ref, bout_ref, o_ref, halo, sbuf, kbuf, vbuf, ybuf, *, ts):
    first = pl.program_id(1) == 0
    n_chunks = CONV_CH // LANES
    n_blocks = ts // BLOCK

    def store_u(row0, val):
        for c in range(n_chunks):
            sbuf[c, pl.ds(2 * row0, val.shape[0], stride=2), :] = val[:, c * LANES:(c + 1) * LANES]

    store_u(0, jnp.where(first, 0.0, halo[...]))
    tap0 = CONV_HALO - (CONV_KERNEL - 1)

    def conv_rows(r0):
        accs = [jnp.broadcast_to(cb_ref[:, c * LANES:(c + 1) * LANES], (CONV_ROWS, LANES)) for c in range(n_chunks)]
        for j in range(CONV_KERNEL):
            for c in range(n_chunks):
                accs[c] = accs[c] + (cw_ref[j:j + 1, c * LANES:(c + 1) * LANES]
                                     * sbuf[c, pl.ds(2 * (r0 + tap0 + j), CONV_ROWS, stride=2), :])
        acc = jnp.concatenate(accs, axis=1)
        mu = jnp.mean(acc, axis=-1, keepdims=True)
        xc = acc - mu
        var = jnp.mean(xc * xc, axis=-1, keepdims=True)
        yn = xc * lax.rsqrt(var + EPS) * lng_ref[...] + lnb_ref[...]
        ybuf[r0:r0 + CONV_ROWS, 0:CONV_CH] = _silu(yn).astype(BF16)

    for buf in (kbuf, vbuf):
        buf[0:BLOCK, :] = jnp.where(first, jnp.zeros((BLOCK, 4 * LANES), BF16), buf[ts:ts + BLOCK, :])

    low_o = lax.broadcasted_iota(I32, (BLOCK, LANES), 1) < HEAD_DIM

    def store_kv(n, kv):
        rows = slice((n + 1) * BLOCK, (n + 2) * BLOCK)
        for src, buf in ((kv[:, 0:LANES], kbuf), (kv[:, LANES:2 * LANES], vbuf)):
            swapped = pltpu.roll(src, HEAD_DIM, axis=1)
            zero = jnp.zeros_like(src)
            buf[rows, 0 * LANES:1 * LANES] = jnp.where(low_o, src, zero).astype(BF16)
            buf[rows, 1 * LANES:2 * LANES] = jnp.where(low_o, zero, swapped).astype(BF16)
            buf[rows, 2 * LANES:3 * LANES] = jnp.where(low_o, swapped, zero).astype(BF16)
            buf[rows, 3 * LANES:4 * LANES] = jnp.where(low_o, zero, src).astype(BF16)

    qi = lax.broadcasted_iota(I32, (BLOCK, 2 * BLOCK), 0)
    kj = lax.broadcasted_iota(I32, (BLOCK, 2 * BLOCK), 1)
    band = (kj > qi) & (kj <= qi + BLOCK)
    band_first = band & ((kj >= BLOCK) | jnp.logical_not(first))

    def attend(n, q):
        rows = slice(n * BLOCK, (n + 2) * BLOCK)
        mask = band_first if n == 0 else band
        for hkv in range(N_KV_HEADS):
            k_bd = jnp.concatenate([kbuf[rows, (2 * hkv) * LANES:(2 * hkv + 1) * LANES],
                                    kbuf[rows, (2 * hkv + 1) * LANES:(2 * hkv + 2) * LANES]], axis=0)
            v_bd = jnp.concatenate([vbuf[rows, (2 * hkv) * LANES:(2 * hkv + 1) * LANES],
                                    vbuf[rows, (2 * hkv + 1) * LANES:(2 * hkv + 2) * LANES]], axis=0)
            for pair in range(2):
                hp = 2 * hkv + pair
                qp = q[:, hp * LANES:(hp + 1) * LANES]
                s = lax.dot_general(qp, k_bd, (((1,), (1,)), ((), ())),
                                    preferred_element_type=F32)
                ps, rden = [], []
                for hh in range(2):
                    sink = sink_ref[2 * hp + hh]
                    sh = jnp.where(mask, s[:, hh * 2 * BLOCK:(hh + 1) * 2 * BLOCK], -jnp.inf)
                    m = jnp.maximum(jnp.max(sh, axis=-1, keepdims=True), sink)
                    p = jnp.exp(sh - m)
                    den = jnp.sum(p, axis=-1, keepdims=True) + jnp.exp(sink - m)
                    ps.append(p.astype(BF16))
                    rden.append(1.0 / den)
                o = jnp.dot(jnp.concatenate(ps, axis=1), v_bd, preferred_element_type=F32)
                o = o * jnp.where(low_o, rden[0], rden[1])
                ybuf[n * BLOCK:(n + 1) * BLOCK, CONV_CH + hp * LANES:CONV_CH + (hp + 1) * LANES] = o.astype(BF16)

    x = x_ref[0]
    h = _rms(x, g_ref[...]).astype(BF16)
    ag = jnp.dot(h, win_ref[:, 0:COL_Q], preferred_element_type=F32) + bin_ref[:, 0:COL_Q]
    u = ag[:, :CONV_CH] * jax.nn.sigmoid(ag[:, CONV_CH:])
    store_u(CONV_HALO, u)
    halo[...] = u[ts - CONV_HALO:, :]
    q = jnp.dot(h, win_ref[:, COL_Q:COL_K], preferred_element_type=F32) + bin_ref[:, COL_Q:COL_K]
    q = (q * (HEAD_DIM ** -0.5)).astype(BF16)
    kv = jnp.dot(h, win_ref[:, COL_K:IN_COLS], preferred_element_type=F32) + bin_ref[:, COL_K:IN_COLS]

    for n in range(n_blocks):
        blk = slice(n * BLOCK, (n + 1) * BLOCK)
        store_kv(n, kv[blk, :])
        attend(n, q[blk, :])
        for r0 in range(n * BLOCK, (n + 1) * BLOCK, CONV_ROWS):
            conv_rows(r0)
        y = jnp.dot(ybuf[blk, :], wout_ref[...], preferred_element_type=F32)
        o_ref[0, blk, :] = x[blk, :] + y + bout_ref[...]


def _mixer(x, g, w_in, b_in, conv_w, conv_b, ln_g, ln_b, sinks, w_out, b_out, *, ts):
    B, S, D = x.shape
    row = lambda a: a.reshape(1, -1)
    const = lambda shape: pl.BlockSpec(shape, lambda b, s: (0,) * len(shape))
    return pl.pallas_call(
        functools.partial(_mixer_kernel, ts=ts),
        grid=(B, S // ts),
        in_specs=[
            pl.BlockSpec((1, ts, D), lambda b, s: (b, s, 0)),
            const((1, D)),
            const((D, IN_COLS)),
            const((1, IN_COLS)),
            const((CONV_KERNEL, CONV_CH)),
            const((1, CONV_CH)),
            const((1, CONV_CH)),
            const((1, CONV_CH)),
            pl.BlockSpec(memory_space=pltpu.SMEM),
            const((D, D)),
            const((1, D)),
        ],
        out_specs=pl.BlockSpec((1, ts, D), lambda b, s: (b, s, 0)),
        out_shape=jax.ShapeDtypeStruct((B, S, D), F32),
        scratch_shapes=[
            pltpu.VMEM((CONV_HALO, CONV_CH), F32),
            pltpu.VMEM((CONV_CH // LANES, 2 * (CONV_HALO + ts), LANES), F32),
            pltpu.VMEM((BLOCK + ts, 4 * LANES), BF16),
            pltpu.VMEM((BLOCK + ts, 4 * LANES), BF16),
            pltpu.VMEM((ts, D), BF16),
        ],
        compiler_params=pltpu.CompilerParams(
            dimension_semantics=("arbitrary", "arbitrary"), vmem_limit_bytes=VMEM_LIMIT),
        name="mixer",
    )(x, row(g), w_in.astype(BF16), row(b_in), conv_w, row(conv_b), row(ln_g), row(ln_b),
      sinks, w_out.astype(BF16), row(b_out))


def _ffn_kernel(x_ref, g_ref, wg_ref, wu_ref, wd_ref, o_ref):
    x = x_ref[...]
    h = _rms(x, g_ref[...]).astype(BF16)
    gate = jnp.dot(h, wg_ref[...], preferred_element_type=F32)
    up = jnp.dot(h, wu_ref[...], preferred_element_type=F32)
    a = (_silu(gate) * up).astype(BF16)
    o_ref[...] = x + jnp.dot(a, wd_ref[...], preferred_element_type=F32)


def _ffn(x, g, w_gate, w_up, w_down, *, tm):
    T, D = x.shape
    F = w_gate.shape[1]
    resident = lambda shape: pl.BlockSpec(shape, lambda i: (0, 0), pipeline_mode=pl.Buffered(1))
    return pl.pallas_call(
        _ffn_kernel,
        grid=(T // tm,),
        in_specs=[
            pl.BlockSpec((tm, D), lambda i: (i, 0)),
            pl.BlockSpec((1, D), lambda i: (0, 0)),
            resident((D, F)),
            resident((D, F)),
            resident((F, D)),
        ],
        out_specs=pl.BlockSpec((tm, D), lambda i: (i, 0)),
        out_shape=jax.ShapeDtypeStruct((T, D), F32),
        compiler_params=pltpu.CompilerParams(dimension_semantics=("arbitrary",), vmem_limit_bytes=VMEM_LIMIT),
        name="dense_ffn",
    )(x, g.reshape(1, D), w_gate.astype(BF16), w_up.astype(BF16), w_down.astype(BF16))


def _stage_store(stage_ref, val, accumulate=False):
    n = val.shape[0]
    for c in range(ROW_SUB):
        tile = val[:, c * LANES:(c + 1) * LANES].reshape(n // SUBLANES, SUBLANES, LANES)
        if accumulate:
            stage_ref[:, c * SUBLANES:(c + 1) * SUBLANES, :] += tile
        else:
            stage_ref[:, c * SUBLANES:(c + 1) * SUBLANES, :] = tile


def _stage_load(stage_ref):
    n = stage_ref.shape[0] * SUBLANES
    return jnp.concatenate(
        [stage_ref[:, c * SUBLANES:(c + 1) * SUBLANES, :].reshape(n, LANES) for c in range(ROW_SUB)], axis=1)


def _stage_to_rows(stage_ref, rows_ref):
    for g in range(stage_ref.shape[0]):
        for t in range(SUBLANES):
            rows_ref[g * SUBLANES + t] = stage_ref[g, pl.ds(t, ROW_SUB, stride=SUBLANES), :]


def _rows_to_stage(rows_ref, stage_ref):
    for g in range(stage_ref.shape[0]):
        for t in range(SUBLANES):
            stage_ref[g, pl.ds(t, ROW_SUB, stride=SUBLANES), :] = rows_ref[g * SUBLANES + t]


def _split_bf16(a):
    hi = a.astype(BF16)
    lo = (a - hi.astype(F32)).astype(BF16)
    return hi, lo


def _router_kernel(x_ref, g_ref, wr_cat_ref, h3_ref, keys_ref, wts_ref, cnt_ref, stage, count, earlier):
    tm = x_ref.shape[0]
    lane = lax.broadcasted_iota(I32, (tm, LANES), 1)

    @pl.when(pl.program_id(0) == 0)
    def _():
        count[...] = jnp.zeros_like(count)
        r = lax.broadcasted_iota(I32, (tm, tm), 0)
        c = lax.broadcasted_iota(I32, (tm, tm), 1)
        earlier[...] = jnp.where(c < r, 1.0, 0.0).astype(BF16)

    hf = _rms(x_ref[...], g_ref[...])
    h_hi, h_lo = _split_bf16(hf)
    hi_terms = jnp.dot(h_hi, wr_cat_ref[...], preferred_element_type=F32)
    logits = (hi_terms[:, :LANES] + hi_terms[:, LANES:]
              + jnp.dot(h_lo, wr_cat_ref[:, :LANES], preferred_element_type=F32))
    lg = jnp.where(lane < N_EXPERTS, logits, -jnp.inf)
    v1 = jnp.max(lg, axis=-1, keepdims=True)
    i1 = jnp.min(jnp.where(lg == v1, lane, LANES), axis=-1, keepdims=True)
    lg2 = jnp.where(lane == i1, -jnp.inf, lg)
    v2 = jnp.max(lg2, axis=-1, keepdims=True)
    i2 = jnp.min(jnp.where(lg2 == v2, lane, LANES), axis=-1, keepdims=True)
    t = jnp.exp(v2 - v1)
    w1 = 1.0 / (1.0 + t)
    w2 = t / (1.0 + t)

    onehot = jnp.where((lane == i1) | (lane == i2), 1.0, 0.0)
    before = count[...] + jnp.dot(earlier[...], onehot.astype(BF16), preferred_element_type=F32)
    rank1 = jnp.sum(jnp.where(lane == i1, before, 0.0), axis=-1, keepdims=True).astype(I32)
    rank2 = jnp.sum(jnp.where(lane == i2, before, 0.0), axis=-1, keepdims=True).astype(I32)
    count[...] += jnp.sum(onehot, axis=0, keepdims=True)

    key1 = i1 * (1 << KEY_SHIFT) + rank1
    key2 = i2 * (1 << KEY_SHIFT) + rank2
    keys = jnp.where(lane == 0, key1.astype(F32), jnp.where(lane == 1, key2.astype(F32), 0.0))
    keys_ref[...] = keys.T[0:SUBLANES, :]
    wts_ref[...] = jnp.where(lane == 0, w1, jnp.where(lane == 1, w2, 0.0))
    cnt_ref[...] = count[...].astype(I32)
    _stage_store(stage, hf)
    _stage_to_rows(stage, h3_ref)


def _router(x, g, w_router, *, tm):
    T, D = x.shape
    wr = jnp.pad(w_router, ((0, 0), (0, LANES - N_EXPERTS)))
    wr_hi = wr.astype(BF16)
    wr_lo = (wr - wr_hi.astype(F32)).astype(BF16)
    return pl.pallas_call(
        _router_kernel,
        grid=(T // tm,),
        in_specs=[
            pl.BlockSpec((tm, D), lambda i: (i, 0)),
            pl.BlockSpec((1, D), lambda i: (0, 0)),
            pl.BlockSpec((D, 2 * LANES), lambda i: (0, 0)),
        ],
        out_specs=[
            pl.BlockSpec((tm, ROW_SUB, LANES), lambda i: (i, 0, 0)),
            pl.BlockSpec((SUBLANES, tm), lambda i: (i, 0)),
            pl.BlockSpec((tm, LANES), lambda i: (i, 0)),
            pl.BlockSpec((1, LANES), lambda i: (0, 0)),
        ],
        out_shape=[
            jax.ShapeDtypeStruct((T, ROW_SUB, LANES), F32),
            jax.ShapeDtypeStruct((T // tm * SUBLANES, tm), F32),
            jax.ShapeDtypeStruct((T, LANES), F32),
            jax.ShapeDtypeStruct((1, LANES), I32),
        ],
        scratch_shapes=[pltpu.VMEM((tm // SUBLANES, ROW_SUB * SUBLANES, LANES), F32),
                        pltpu.VMEM((1, LANES), F32),
                        pltpu.VMEM((tm, tm), BF16)],
        compiler_params=pltpu.CompilerParams(dimension_semantics=("arbitrary",), vmem_limit_bytes=VMEM_LIMIT),
        name="router",
    )(x, g.reshape(1, D), jnp.concatenate([wr_hi, wr_lo], axis=1))


def _dispatch_kernel(pos1_ref, pos2_ref, pad_start_ref, pad_len_ref, nu_ref, h3_ref, xs_ref,
                     zeros, sem, zsem, *, tmg):
    tm = h3_ref.shape[0]
    base = pl.program_id(0) * tm

    @pl.when(pl.program_id(0) == 0)
    def _():
        zeros[...] = jnp.zeros_like(zeros)

        def pad_copies(act):
            for e in range(N_EXPERTS):
                start, length = pad_start_ref[e], pad_len_ref[e]
                for b in range(tmg.bit_length() - 1):
                    size = 1 << b
                    offset = start + ((length >> (b + 1)) << (b + 1))
                    copy = pltpu.make_async_copy(zeros.at[pl.ds(0, size)], xs_ref.at[pl.ds(offset, size)], zsem)
                    pl.when(((length >> b) & 1) == 1)(lambda copy=copy: act(copy))

        def tile_copies(act):
            def body(i, carry):
                act(pltpu.make_async_copy(zeros, xs_ref.at[pl.ds(i * tmg, tmg)], zsem))
                return carry

            lax.fori_loop(nu_ref[0], xs_ref.shape[0] // tmg, body, 0)

        for act in (lambda c: c.start(), lambda c: c.wait()):
            pad_copies(act)
            tile_copies(act)

    def issue(blk, carry):
        for u in range(DMA_UNROLL):
            t = blk * DMA_UNROLL + u
            for k, pos_ref in enumerate((pos1_ref, pos2_ref)):
                pltpu.make_async_copy(h3_ref.at[t], xs_ref.at[pos_ref[base + t]], sem).start(priority=k)
        return carry

    lax.fori_loop(0, tm // DMA_UNROLL, issue, 0)
    for _ in range(2):
        pltpu.make_async_copy(h3_ref, xs_ref.at[pl.ds(0, tm)], sem).wait()


def _dispatch(pos1, pos2, pad_start, pad_len, n_used, h3, n_rows, *, tm, tmg):
    T = h3.shape[0]
    return pl.pallas_call(
        functools.partial(_dispatch_kernel, tmg=tmg),
        grid_spec=pltpu.PrefetchScalarGridSpec(
            num_scalar_prefetch=5,
            grid=(T // tm,),
            in_specs=[pl.BlockSpec((tm, ROW_SUB, LANES), lambda i, *_: (i, 0, 0))],
            out_specs=pl.BlockSpec(memory_space=pl.ANY),
            scratch_shapes=[pltpu.VMEM((tmg, ROW_SUB, LANES), F32),
                            pltpu.SemaphoreType.DMA, pltpu.SemaphoreType.DMA],
        ),
        out_shape=jax.ShapeDtypeStruct((n_rows, ROW_SUB, LANES), F32),
        compiler_params=pltpu.CompilerParams(dimension_semantics=("arbitrary",)),
        name="dispatch",
    )(pos1, pos2, pad_start, pad_len, n_used, h3)


def _expert_kernel(te_ref, nu_ref, x3_ref, wg_ref, wu_ref, wd_ref, y3_ref, stage, h_scr):
    del te_ref
    j = pl.program_id(1)
    used = pl.program_id(0) < nu_ref[0]

    @pl.when(jnp.logical_not(used) & (j == 0))
    def _():
        y3_ref[...] = jnp.zeros_like(y3_ref)

    @pl.when(used)
    def _():
        @pl.when(j == 0)
        def _():
            _rows_to_stage(x3_ref, stage)
            h_scr[...] = _stage_load(stage).astype(BF16)
            stage[...] = jnp.zeros_like(stage)

        h = h_scr[...]
        gate = jnp.dot(h, wg_ref[0], preferred_element_type=F32)
        up = jnp.dot(h, wu_ref[0], preferred_element_type=F32)
        act = (_silu(gate) * up).astype(BF16)
        _stage_store(stage, jnp.dot(act, wd_ref[0], preferred_element_type=F32), accumulate=True)

        @pl.when(j == pl.num_programs(1) - 1)
        def _():
            _stage_to_rows(stage, y3_ref)


def _experts(tile_end, n_used, xs3, w_gate, w_up, w_down, *, tmg, tf):
    n_rows = xs3.shape[0]
    E, D, F = w_gate.shape
    nj = F // tf
    row_blk = lambda i, j, te, nu: (jnp.minimum(i, nu[0] - 1), 0, 0)

    def expert(i, te, nu):
        ii = jnp.minimum(i, nu[0] - 1)
        return sum((ii >= te[e]).astype(I32) for e in range(E - 1))

    def chunk(i, j, nu):
        serp = lambda ii, jj: jnp.where(ii % 2 == 0, jj, nj - 1 - jj)
        return jnp.where(i < nu[0], serp(i, j), serp(nu[0] - 1, nj - 1))

    return pl.pallas_call(
        _expert_kernel,
        grid_spec=pltpu.PrefetchScalarGridSpec(
            num_scalar_prefetch=2,
            grid=(n_rows // tmg, nj),
            in_specs=[
                pl.BlockSpec((tmg, ROW_SUB, LANES), row_blk),
                pl.BlockSpec((1, D, tf), lambda i, j, te, nu: (expert(i, te, nu), 0, chunk(i, j, nu))),
                pl.BlockSpec((1, D, tf), lambda i, j, te, nu: (expert(i, te, nu), 0, chunk(i, j, nu))),
                pl.BlockSpec((1, tf, D), lambda i, j, te, nu: (expert(i, te, nu), chunk(i, j, nu), 0)),
            ],
            out_specs=pl.BlockSpec((tmg, ROW_SUB, LANES), lambda i, j, te, nu: (i, 0, 0)),
            scratch_shapes=[pltpu.VMEM((tmg // SUBLANES, ROW_SUB * SUBLANES, LANES), F32),
                            pltpu.VMEM((tmg, D), BF16)],
        ),
        out_shape=jax.ShapeDtypeStruct((n_rows, ROW_SUB, LANES), F32),
        compiler_params=pltpu.CompilerParams(
            dimension_semantics=("arbitrary", "arbitrary"), vmem_limit_bytes=VMEM_LIMIT),
        name="experts",
    )(tile_end, n_used, xs3, w_gate.astype(BF16), w_up.astype(BF16), w_down.astype(BF16))


def _combine_kernel(pos1_ref, pos2_ref, x_ref, wts_ref, fg_ref, y3_ref, o_ref, rows, stage, sems):
    tm = x_ref.shape[0]
    i = pl.program_id(0)

    def issue(tile, slot):
        base = tile * tm

        def body(blk, carry):
            for u in range(DMA_UNROLL):
                t = blk * DMA_UNROLL + u
                for k, pos_ref in enumerate((pos1_ref, pos2_ref)):
                    pltpu.make_async_copy(y3_ref.at[pos_ref[base + t]], rows.at[slot, k, t],
                                          sems.at[slot]).start(priority=k)
            return carry

        lax.fori_loop(0, tm // DMA_UNROLL, body, 0)

    @pl.when(i == 0)
    def _():
        issue(0, 0)

    @pl.when(i + 1 < pl.num_programs(0))
    def _():
        issue(i + 1, (i + 1) % 2)

    slot = i % 2
    for k in range(2):
        pltpu.make_async_copy(y3_ref.at[pl.ds(0, tm)], rows.at[slot, k], sems.at[slot]).wait()

    w = wts_ref[...]
    _rows_to_stage(rows.at[slot, 0], stage)
    y1 = _stage_load(stage) * w[:, 0:1]
    _rows_to_stage(rows.at[slot, 1], stage)
    y2 = _stage_load(stage) * w[:, 1:2]
    o_ref[...] = _rms(x_ref[...] + (y1 + y2), fg_ref[...])


def _combine(pos1, pos2, x, wts, final_g, y3, *, tm):
    T, D = x.shape
    return pl.pallas_call(
        _combine_kernel,
        grid_spec=pltpu.PrefetchScalarGridSpec(
            num_scalar_prefetch=2,
            grid=(T // tm,),
            in_specs=[
                pl.BlockSpec((tm, D), lambda i, *_: (i, 0)),
                pl.BlockSpec((tm, LANES), lambda i, *_: (i, 0)),
                pl.BlockSpec((1, D), lambda i, *_: (0, 0)),
                pl.BlockSpec(memory_space=pl.ANY),
            ],
            out_specs=pl.BlockSpec((tm, D), lambda i, *_: (i, 0)),
            scratch_shapes=[pltpu.VMEM((2, 2, tm, ROW_SUB, LANES), F32),
                            pltpu.VMEM((tm // SUBLANES, ROW_SUB * SUBLANES, LANES), F32),
                            pltpu.SemaphoreType.DMA((2,))],
        ),
        out_shape=jax.ShapeDtypeStruct((T, D), F32),
        compiler_params=pltpu.CompilerParams(dimension_semantics=("arbitrary",), vmem_limit_bytes=VMEM_LIMIT),
        name="combine",
    )(pos1, pos2, x, wts, final_g.reshape(1, D), y3)


def _moe(x, g, w_router, w_gate, w_up, w_down, final_g, *, tm, tmg, tf):
    T, D = x.shape
    E = N_EXPERTS
    h3, keys, wts, cnt = _router(x, g, w_router, tm=tm)
    keys = keys.reshape(T // tm, SUBLANES, tm).astype(I32)
    key1, key2 = keys[:, 0, :].reshape(T), keys[:, 1, :].reshape(T)

    counts = cnt[0, :E]
    tiles = (counts + (tmg - 1)) // tmg
    tile_end = jnp.cumsum(tiles).astype(I32)
    starts = ((tile_end - tiles) * tmg).astype(I32)
    n_tiles = 2 * T // tmg + E
    n_used = tile_end[-1:]
    pad_start = (starts + counts).astype(I32)
    pad_len = (tiles * tmg - counts).astype(I32)
    def position(key):
        pos = key & KEY_MASK
        for e in range(E):
            pos = pos + jnp.where((key >> KEY_SHIFT) == e, starts[e], 0)
        return pos

    pos1, pos2 = position(key1), position(key2)

    xs3 = _dispatch(pos1, pos2, pad_start, pad_len, n_used, h3, n_tiles * tmg, tm=tm, tmg=tmg)
    y3 = _experts(tile_end, n_used, xs3, w_gate, w_up, w_down, tmg=tmg, tf=tf)
    return _combine(pos1, pos2, x, wts, final_g, y3, tm=tm)


def kernel(x, attn_norm, ffn_norm, w_in, b_in, conv_w, conv_b, conv_ln_g, conv_ln_b, sinks, w_out, b_out,
           ffn_w_gate, ffn_w_up, ffn_w_down, moe_router, moe_w_gate, moe_w_up, moe_w_down, final_norm):
    B, S, D = x.shape
    ts = min(S, 512)
    tm = min(B * S, 512)

    def mixer(x, l):
        return _mixer(x, attn_norm[l], w_in[l], b_in[l], conv_w[l], conv_b[l], conv_ln_g[l], conv_ln_b[l],
                      sinks[l], w_out[l], b_out[l], ts=ts)

    x = mixer(x, 0)
    x = _ffn(x.reshape(B * S, D), ffn_norm[0], ffn_w_gate[0], ffn_w_up[0], ffn_w_down[0],
             tm=tm)
    x = mixer(x.reshape(B, S, D), 1)
    x = _moe(x.reshape(B * S, D), ffn_norm[1], moe_router[0], moe_w_gate[0], moe_w_up[0], moe_w_down[0],
             final_norm, tm=tm, tmg=tm, tf=moe_w_gate.shape[3] // 2)
    return x.reshape(B, S, D)
```

```python
import functools

import jax
import jax.numpy as jnp
from jax import lax
from jax.experimental import pallas as pl
from jax.experimental.pallas import tpu as pltpu

F32 = jnp.float32
BF16 = jnp.bfloat16
I32 = jnp.int32

D_MODEL = 1024
CONV_CH = 512
CONV_KERNEL = 31
HEAD_DIM = 64
N_Q_HEADS = 8
N_KV_HEADS = 2
ATTN_WIDTH = N_Q_HEADS * HEAD_DIM
KV_WIDTH = N_KV_HEADS * HEAD_DIM
BLOCK = 128
N_EXPERTS = 8
EPS = 1e-5

LANES = 128
SUBLANES = 8
CONV_HALO = 32
CONV_ROWS = 32
VMEM_LIMIT = 56 * 1024 * 1024

COL_Q = 2 * CONV_CH
COL_K = COL_Q + ATTN_WIDTH
COL_V = COL_K + KV_WIDTH
IN_COLS = COL_V + KV_WIDTH

ROW_SUB = D_MODEL // LANES
KEY_SHIFT = 20
KEY_MASK = (1 << KEY_SHIFT) - 1
DMA_UNROLL = 8


def _rms(x, g):
    ms = jnp.mean(x * x, axis=-1, keepdims=True)
    return x * lax.rsqrt(ms + EPS) * g


def _silu(x):
    return x * jax.nn.sigmoid(x)


def _mixer_kernel(x_ref, g_ref, win_ref, bin_ref, cw_ref, cb_ref, lng_ref, lnb_ref,
                  sink_ref, wout_ref, bout_ref, o_ref, halo, sbuf, kbuf, vbuf, ybuf, *, ts):
    first = pl.program_id(1) == 0
    n_chunks = CONV_CH // LANES
    n_blocks = ts // BLOCK

    def store_u(row0, val):
        for c in range(n_chunks):
            sbuf[c, pl.ds(2 * row0, val.shape[0], stride=2), :] = val[:, c * LANES:(c + 1) * LANES]

    store_u(0, jnp.where(first, 0.0, halo[...]))
    tap0 = CONV_HALO - (CONV_KERNEL - 1)

    def conv_rows(r0):
        accs = [jnp.broadcast_to(cb_ref[:, c * LANES:(c + 1) * LANES], (CONV_ROWS, LANES)) for c in range(n_chunks)]
        for j in range(CONV_KERNEL):
            for c in range(n_chunks):
                accs[c] = accs[c] + (cw_ref[j:j + 1, c * LANES:(c + 1) * LANES]
                                     * sbuf[c, pl.ds(2 * (r0 + tap0 + j), CONV_ROWS, stride=2), :])
        acc = jnp.concatenate(accs, axis=1)
        mu = jnp.mean(acc, axis=-1, keepdims=True)
        xc = acc - mu
        var = jnp.mean(xc * xc, axis=-1, keepdims=True)
        yn = xc * lax.rsqrt(var + EPS) * lng_ref[...] + lnb_ref[...]
        ybuf[r0:r0 + CONV_ROWS, 0:CONV_CH] = _silu(yn).astype(BF16)

    for buf in (kbuf, vbuf):
        buf[0:BLOCK, :] = jnp.where(first, jnp.zeros((BLOCK, 4 * LANES), BF16), buf[ts:ts + BLOCK, :])

    low_o = lax.broadcasted_iota(I32, (BLOCK, LANES), 1) < HEAD_DIM

    def store_kv(n, kv):
        rows = slice((n + 1) * BLOCK, (n + 2) * BLOCK)
        for src, buf in ((kv[:, 0:LANES], kbuf), (kv[:, LANES:2 * LANES], vbuf)):
            swapped = pltpu.roll(src, HEAD_DIM, axis=1)
            zero = jnp.zeros_like(src)
            buf[rows, 0 * LANES:1 * LANES] = jnp.where(low_o, src, zero).astype(BF16)
            buf[rows, 1 * LANES:2 * LANES] = jnp.where(low_o, zero, swapped).astype(BF16)
            buf[rows, 2 * LANES:3 * LANES] = jnp.where(low_o, swapped, zero).astype(BF16)
            buf[rows, 3 * LANES:4 * LANES] = jnp.where(low_o, zero, src).astype(BF16)

    qi = lax.broadcasted_iota(I32, (BLOCK, 2 * BLOCK), 0)
    kj = lax.broadcasted_iota(I32, (BLOCK, 2 * BLOCK), 1)
    band = (kj > qi) & (kj <= qi + BLOCK)
    band_first = band & ((kj >= BLOCK) | jnp.logical_not(first))

    def attend(n, q):
        rows = slice(n * BLOCK, (n + 2) * BLOCK)
        mask = band_first if n == 0 else band
        for hkv in range(N_KV_HEADS):
            k_bd = jnp.concatenate([kbuf[rows, (2 * hkv) * LANES:(2 * hkv + 1) * LANES],
                                    kbuf[rows, (2 * hkv + 1) * LANES:(2 * hkv + 2) * LANES]], axis=0)
            v_bd = jnp.concatenate([vbuf[rows, (2 * hkv) * LANES:(2 * hkv + 1) * LANES],
                                    vbuf[rows, (2 * hkv + 1) * LANES:(2 * hkv + 2) * LANES]], axis=0)
            for pair in range(2):
                hp = 2 * hkv + pair
                qp = q[:, hp * LANES:(hp + 1) * LANES]
                s = lax.dot_general(qp, k_bd, (((1,), (1,)), ((), ())),
                                    preferred_element_type=F32)
                ps, rden = [], []
                for hh in range(2):
                    sink = sink_ref[2 * hp + hh]
                    sh = jnp.where(mask, s[:, hh * 2 * BLOCK:(hh + 1) * 2 * BLOCK], -jnp.inf)
                    m = jnp.maximum(jnp.max(sh, axis=-1, keepdims=True), sink)
                    p = jnp.exp(sh - m)
                    den = jnp.sum(p, axis=-1, keepdims=True) + jnp.exp(sink - m)
                    ps.append(p.astype(BF16))
                    rden.append(1.0 / den)
                o = jnp.dot(jnp.concatenate(ps, axis=1), v_bd, preferred_element_type=F32)
                o = o * jnp.where(low_o, rden[0], rden[1])
                ybuf[n * BLOCK:(n + 1) * BLOCK, CONV_CH + hp * LANES:CONV_CH + (hp + 1) * LANES] = o.astype(BF16)

    x = x_ref[0]
    h = _rms(x, g_ref[...]).astype(BF16)
    ag = jnp.dot(h, win_ref[:, 0:COL_Q], preferred_element_type=F32) + bin_ref[:, 0:COL_Q]
    u = ag[:, :CONV_CH] * jax.nn.sigmoid(ag[:, CONV_CH:])
    store_u(CONV_HALO, u)
    halo[...] = u[ts - CONV_HALO:, :]
    q = jnp.dot(h, win_ref[:, COL_Q:COL_K], preferred_element_type=F32) + bin_ref[:, COL_Q:COL_K]
    q = (q * (HEAD_DIM ** -0.5)).astype(BF16)
    kv = jnp.dot(h, win_ref[:, COL_K:IN_COLS], preferred_element_type=F32) + bin_ref[:, COL_K:IN_COLS]

    for n in range(n_blocks):
        blk = slice(n * BLOCK, (n + 1) * BLOCK)
        store_kv(n, kv[blk, :])
        attend(n, q[blk, :])
        for r0 in range(n * BLOCK, (n + 1) * BLOCK, CONV_ROWS):
            conv_rows(r0)
        y = jnp.dot(ybuf[blk, :], wout_ref[...], preferred_element_type=F32)
        o_ref[0, blk, :] = x[blk, :] + y + bout_ref[...]


def _mixer(x, g, w_in, b_in, conv_w, conv_b, ln_g, ln_b, sinks, w_out, b_out, *, ts):
    B, S, D = x.shape
    row = lambda a: a.reshape(1, -1)
    const = lambda shape: pl.BlockSpec(shape, lambda b, s: (0,) * len(shape))
    return pl.pallas_call(
        functools.partial(_mixer_kernel, ts=ts),
        grid=(B, S // ts),
        in_specs=[
            pl.BlockSpec((1, ts, D), lambda b, s: (b, s, 0)),
            const((1, D)),
            const((D, IN_COLS)),
            const((1, IN_COLS)),
            const((CONV_KERNEL, CONV_CH)),
            const((1, CONV_CH)),
            const((1, CONV_CH)),
            const((1, CONV_CH)),
            pl.BlockSpec(memory_space=pltpu.SMEM),
            const((D, D)),
            const((1, D)),
        ],
        out_specs=pl.BlockSpec((1, ts, D), lambda b, s: (b, s, 0)),
        out_shape=jax.ShapeDtypeStruct((B, S, D), F32),
        scratch_shapes=[
            pltpu.VMEM((CONV_HALO, CONV_CH), F32),
            pltpu.VMEM((CONV_CH // LANES, 2 * (CONV_HALO + ts), LANES), F32),
            pltpu.VMEM((BLOCK + ts, 4 * LANES), BF16),
            pltpu.VMEM((BLOCK + ts, 4 * LANES), BF16),
            pltpu.VMEM((ts, D), BF16),
        ],
        compiler_params=pltpu.CompilerParams(
            dimension_semantics=("arbitrary", "arbitrary"), vmem_limit_bytes=VMEM_LIMIT),
        name="mixer",
    )(x, row(g), w_in.astype(BF16), row(b_in), conv_w, row(conv_b), row(ln_g), row(ln_b),
      sinks, w_out.astype(BF16), row(b_out))


def _ffn_kernel(x_ref, g_ref, wg_ref, wu_ref, wd_ref, o_ref):
    x = x_ref[...]
    h = _rms(x, g_ref[...]).astype(BF16)
    gate = jnp.dot(h, wg_ref[...], preferred_element_type=F32)
    up = jnp.dot(h, wu_ref[...], preferred_element_type=F32)
    a = (_silu(gate) * up).astype(BF16)
    o_ref[...] = x + jnp.dot(a, wd_ref[...], preferred_element_type=F32)


def _ffn(x, g, w_gate, w_up, w_down, *, tm):
    T, D = x.shape
    F = w_gate.shape[1]
    resident = lambda shape: pl.BlockSpec(shape, lambda i: (0, 0), pipeline_mode=pl.Buffered(1))
    return pl.pallas_call(
        _ffn_kernel,
        grid=(T // tm,),
        in_specs=[
            pl.BlockSpec((tm, D), lambda i: (i, 0)),
            pl.BlockSpec((1, D), lambda i: (0, 0)),
            resident((D, F)),
            resident((D, F)),
            resident((F, D)),
        ],
        out_specs=pl.BlockSpec((tm, D), lambda i: (i, 0)),
        out_shape=jax.ShapeDtypeStruct((T, D), F32),
        compiler_params=pltpu.CompilerParams(dimension_semantics=("arbitrary",), vmem_limit_bytes=VMEM_LIMIT),
        name="dense_ffn",
    )(x, g.reshape(1, D), w_gate.astype(BF16), w_up.astype(BF16), w_down.astype(BF16))


def _stage_store(stage_ref, val, accumulate=None):
    n = val.shape[0]
    for c in range(ROW_SUB):
        tile = val[:, c * LANES:(c + 1) * LANES].reshape(n // SUBLANES, SUBLANES, LANES)
        rows = slice(c * SUBLANES, (c + 1) * SUBLANES)
        if accumulate is not None:
            tile = tile + jnp.where(accumulate, stage_ref[:, rows, :], 0.0)
        stage_ref[:, rows, :] = tile


def _stage_load(stage_ref):
    n = stage_ref.shape[0] * SUBLANES
    return jnp.concatenate(
        [stage_ref[:, c * SUBLANES:(c + 1) * SUBLANES, :].reshape(n, LANES) for c in range(ROW_SUB)], axis=1)


def _stage_to_rows(stage_ref, rows_ref):
    for g in range(stage_ref.shape[0]):
        for t in range(SUBLANES):
            rows_ref[g * SUBLANES + t] = stage_ref[g, pl.ds(t, ROW_SUB, stride=SUBLANES), :]


def _rows_to_stage(rows_ref, stage_ref):
    for g in range(stage_ref.shape[0]):
        for t in range(SUBLANES):
            stage_ref[g, pl.ds(t, ROW_SUB, stride=SUBLANES), :] = rows_ref[g * SUBLANES + t]


def _split_bf16(a):
    hi = a.astype(BF16)
    lo = (a - hi.astype(F32)).astype(BF16)
    return hi, lo


def _router_kernel(x_ref, g_ref, wr_cat_ref, h3_ref, keys_ref, wts_ref, cnt_ref, stage, count, earlier):
    tm = x_ref.shape[0]
    lane = lax.broadcasted_iota(I32, (tm, LANES), 1)

    @pl.when(pl.program_id(0) == 0)
    def _():
        count[...] = jnp.zeros_like(count)
        r = lax.broadcasted_iota(I32, (tm, tm), 0)
        c = lax.broadcasted_iota(I32, (tm, tm), 1)
        earlier[...] = jnp.where(c < r, 1.0, 0.0).astype(BF16)

    hf = _rms(x_ref[...], g_ref[...])
    h_hi, h_lo = _split_bf16(hf)
    hi_terms = jnp.dot(h_hi, wr_cat_ref[...], preferred_element_type=F32)
    logits = (hi_terms[:, :LANES] + hi_terms[:, LANES:]
              + jnp.dot(h_lo, wr_cat_ref[:, :LANES], preferred_element_type=F32))
    lg = jnp.where(lane < N_EXPERTS, logits, -jnp.inf)
    v1 = jnp.max(lg, axis=-1, keepdims=True)
    i1 = jnp.min(jnp.where(lg == v1, lane, LANES), axis=-1, keepdims=True)
    lg2 = jnp.where(lane == i1, -jnp.inf, lg)
    v2 = jnp.max(lg2, axis=-1, keepdims=True)
    i2 = jnp.min(jnp.where(lg2 == v2, lane, LANES), axis=-1, keepdims=True)
    t = jnp.exp(v2 - v1)
    w1 = 1.0 / (1.0 + t)
    w2 = t / (1.0 + t)

    onehot = jnp.where((lane == i1) | (lane == i2), 1.0, 0.0)
    before = count[...] + jnp.dot(earlier[...], onehot.astype(BF16), preferred_element_type=F32)
    rank1 = jnp.sum(jnp.where(lane == i1, before, 0.0), axis=-1, keepdims=True).astype(I32)
    rank2 = jnp.sum(jnp.where(lane == i2, before, 0.0), axis=-1, keepdims=True).astype(I32)
    count[...] += jnp.sum(onehot, axis=0, keepdims=True)

    key1 = i1 * (1 << KEY_SHIFT) + rank1
    key2 = i2 * (1 << KEY_SHIFT) + rank2
    keys = jnp.where(lane == 0, key1.astype(F32), jnp.where(lane == 1, key2.astype(F32), 0.0))
    keys_ref[...] = keys.T[0:SUBLANES, :]
    wts_ref[...] = jnp.where(lane == 0, w1, jnp.where(lane == 1, w2, 0.0))
    cnt_ref[...] = count[...].astype(I32)
    _stage_store(stage, hf)
    _stage_to_rows(stage, h3_ref)


def _router(x, g, w_router, *, tm):
    T, D = x.shape
    wr = jnp.pad(w_router, ((0, 0), (0, LANES - N_EXPERTS)))
    wr_hi = wr.astype(BF16)
    wr_lo = (wr - wr_hi.astype(F32)).astype(BF16)
    return pl.pallas_call(
        _router_kernel,
        grid=(T // tm,),
        in_specs=[
            pl.BlockSpec((tm, D), lambda i: (i, 0)),
            pl.BlockSpec((1, D), lambda i: (0, 0)),
            pl.BlockSpec((D, 2 * LANES), lambda i: (0, 0)),
        ],
        out_specs=[
            pl.BlockSpec((tm, ROW_SUB, LANES), lambda i: (i, 0, 0)),
            pl.BlockSpec((SUBLANES, tm), lambda i: (i, 0)),
            pl.BlockSpec((tm, LANES), lambda i: (i, 0)),
            pl.BlockSpec((1, LANES), lambda i: (0, 0)),
        ],
        out_shape=[
            jax.ShapeDtypeStruct((T, ROW_SUB, LANES), F32),
            jax.ShapeDtypeStruct((T // tm * SUBLANES, tm), F32),
            jax.ShapeDtypeStruct((T, LANES), F32),
            jax.ShapeDtypeStruct((1, LANES), I32),
        ],
        scratch_shapes=[pltpu.VMEM((tm // SUBLANES, ROW_SUB * SUBLANES, LANES), F32),
                        pltpu.VMEM((1, LANES), F32),
                        pltpu.VMEM((tm, tm), BF16)],
        compiler_params=pltpu.CompilerParams(dimension_semantics=("arbitrary",), vmem_limit_bytes=VMEM_LIMIT),
        name="router",
    )(x, g.reshape(1, D), jnp.concatenate([wr_hi, wr_lo], axis=1))


def _dispatch_kernel(pos1_ref, pos2_ref, pad_start_ref, pad_len_ref, nu_ref, h3_ref, xs_ref,
                     zeros, sem, zsem, *, tmg):
    tm = h3_ref.shape[0]
    base = pl.program_id(0) * tm

    @pl.when(pl.program_id(0) == 0)
    def _():
        zeros[...] = jnp.zeros_like(zeros)

        def pad_copies(act):
            for e in range(N_EXPERTS):
                start, length = pad_start_ref[e], pad_len_ref[e]
                for b in range(tmg.bit_length() - 1):
                    size = 1 << b
                    offset = start + ((length >> (b + 1)) << (b + 1))
                    copy = pltpu.make_async_copy(zeros.at[pl.ds(0, size)], xs_ref.at[pl.ds(offset, size)], zsem)
                    pl.when(((length >> b) & 1) == 1)(lambda copy=copy: act(copy))

        def tile_copies(act):
            def body(i, carry):
                act(pltpu.make_async_copy(zeros, xs_ref.at[pl.ds(i * tmg, tmg)], zsem))
                return carry

            lax.fori_loop(nu_ref[0], xs_ref.shape[0] // tmg, body, 0)

        for act in (lambda c: c.start(), lambda c: c.wait()):
            pad_copies(act)
            tile_copies(act)

    def issue(blk, carry):
        for u in range(DMA_UNROLL):
            t = blk * DMA_UNROLL + u
            for k, pos_ref in enumerate((pos1_ref, pos2_ref)):
                pltpu.make_async_copy(h3_ref.at[t], xs_ref.at[pos_ref[base + t]], sem).start(priority=k)
        return carry

    lax.fori_loop(0, tm // DMA_UNROLL, issue, 0)
    for _ in range(2):
        pltpu.make_async_copy(h3_ref, xs_ref.at[pl.ds(0, tm)], sem).wait()


def _dispatch(pos1, pos2, pad_start, pad_len, n_used, h3, n_rows, *, tm, tmg):
    T = h3.shape[0]
    return pl.pallas_call(
        functools.partial(_dispatch_kernel, tmg=tmg),
        grid_spec=pltpu.PrefetchScalarGridSpec(
            num_scalar_prefetch=5,
            grid=(T // tm,),
            in_specs=[pl.BlockSpec((tm, ROW_SUB, LANES), lambda i, *_: (i, 0, 0))],
            out_specs=pl.BlockSpec(memory_space=pl.ANY),
            scratch_shapes=[pltpu.VMEM((tmg, ROW_SUB, LANES), F32),
                            pltpu.SemaphoreType.DMA, pltpu.SemaphoreType.DMA],
        ),
        out_shape=jax.ShapeDtypeStruct((n_rows, ROW_SUB, LANES), F32),
        compiler_params=pltpu.CompilerParams(dimension_semantics=("arbitrary",)),
        name="dispatch",
    )(pos1, pos2, pad_start, pad_len, n_used, h3)


def _expert_kernel(te_ref, nu_ref, x3_ref, wg_ref, wu_ref, wd_ref, y3_ref, stage, h_scr):
    del te_ref
    j = pl.program_id(1)
    used = pl.program_id(0) < nu_ref[0]

    @pl.when(jnp.logical_not(used) & (j == 0))
    def _():
        y3_ref[...] = jnp.zeros_like(y3_ref)

    @pl.when(used)
    def _():
        @pl.when(j == 0)
        def _():
            _rows_to_stage(x3_ref, stage)
            h_scr[...] = _stage_load(stage).astype(BF16)

        h = h_scr[...]
        gate = jnp.dot(h, wg_ref[0], preferred_element_type=F32)
        up = jnp.dot(h, wu_ref[0], preferred_element_type=F32)
        act = (_silu(gate) * up).astype(BF16)
        _stage_store(stage, jnp.dot(act, wd_ref[0], preferred_element_type=F32), accumulate=j > 0)

        @pl.when(j == pl.num_programs(1) - 1)
        def _():
            _stage_to_rows(stage, y3_ref)


def _experts(tile_end, n_used, xs3, w_gate, w_up, w_down, *, tmg, tf):
    n_rows = xs3.shape[0]
    E, D, F = w_gate.shape
    nj = F // tf
    row_blk = lambda i, j, te, nu: (jnp.minimum(i, nu[0] - 1), 0, 0)

    def expert(i, te, nu):
        ii = jnp.minimum(i, nu[0] - 1)
        return sum((ii >= te[e]).astype(I32) for e in range(E - 1))

    def chunk(i, j, nu):
        serp = lambda ii, jj: jnp.where(ii % 2 == 0, jj, nj - 1 - jj)
        return jnp.where(i < nu[0], serp(i, j), serp(nu[0] - 1, nj - 1))

    return pl.pallas_call(
        _expert_kernel,
        grid_spec=pltpu.PrefetchScalarGridSpec(
            num_scalar_prefetch=2,
            grid=(n_rows // tmg, nj),
            in_specs=[
                pl.BlockSpec((tmg, ROW_SUB, LANES), row_blk),
                pl.BlockSpec((1, D, tf), lambda i, j, te, nu: (expert(i, te, nu), 0, chunk(i, j, nu))),
                pl.BlockSpec((1, D, tf), lambda i, j, te, nu: (expert(i, te, nu), 0, chunk(i, j, nu))),
                pl.BlockSpec((1, tf, D), lambda i, j, te, nu: (expert(i, te, nu), chunk(i, j, nu), 0)),
            ],
            out_specs=pl.BlockSpec((tmg, ROW_SUB, LANES), lambda i, j, te, nu: (i, 0, 0)),
            scratch_shapes=[pltpu.VMEM((tmg // SUBLANES, ROW_SUB * SUBLANES, LANES), F32),
                            pltpu.VMEM((tmg, D), BF16)],
        ),
        out_shape=jax.ShapeDtypeStruct((n_rows, ROW_SUB, LANES), F32),
        compiler_params=pltpu.CompilerParams(
            dimension_semantics=("arbitrary", "arbitrary"), vmem_limit_bytes=VMEM_LIMIT),
        name="experts",
    )(tile_end, n_used, xs3, w_gate.astype(BF16), w_up.astype(BF16), w_down.astype(BF16))


def _combine_kernel(pos1_ref, pos2_ref, x_ref, wts_ref, fg_ref, y3_ref, o_ref, rows, stage, sems):
    tm = x_ref.shape[0]
    i = pl.program_id(0)

    def issue(tile, slot):
        base = tile * tm

        def body(blk, carry):
            for u in range(DMA_UNROLL):
                t = blk * DMA_UNROLL + u
                for k, pos_ref in enumerate((pos1_ref, pos2_ref)):
                    pltpu.make_async_copy(y3_ref.at[pos_ref[base + t]], rows.at[slot, k, t],
                                          sems.at[slot]).start(priority=k)
            return carry

        lax.fori_loop(0, tm // DMA_UNROLL, body, 0)

    @pl.when(i == 0)
    def _():
        issue(0, 0)

    @pl.when(i + 1 < pl.num_programs(0))
    def _():
        issue(i + 1, (i + 1) % 2)

    slot = i % 2
    for k in range(2):
        pltpu.make_async_copy(y3_ref.at[pl.ds(0, tm)], rows.at[slot, k], sems.at[slot]).wait()

    w = wts_ref[...]
    _rows_to_stage(rows.at[slot, 0], stage)
    y1 = _stage_load(stage) * w[:, 0:1]
    _rows_to_stage(rows.at[slot, 1], stage)
    y2 = _stage_load(stage) * w[:, 1:2]
    o_ref[...] = _rms(x_ref[...] + (y1 + y2), fg_ref[...])


def _combine(pos1, pos2, x, wts, final_g, y3, *, tm):
    T, D = x.shape
    return pl.pallas_call(
        _combine_kernel,
        grid_spec=pltpu.PrefetchScalarGridSpec(
            num_scalar_prefetch=2,
            grid=(T // tm,),
            in_specs=[
                pl.BlockSpec((tm, D), lambda i, *_: (i, 0)),
                pl.BlockSpec((tm, LANES), lambda i, *_: (i, 0)),
                pl.BlockSpec((1, D), lambda i, *_: (0, 0)),
                pl.BlockSpec(memory_space=pl.ANY),
            ],
            out_specs=pl.BlockSpec((tm, D), lambda i, *_: (i, 0)),
            scratch_shapes=[pltpu.VMEM((2, 2, tm, ROW_SUB, LANES), F32),
                            pltpu.VMEM((tm // SUBLANES, ROW_SUB * SUBLANES, LANES), F32),
                            pltpu.SemaphoreType.DMA((2,))],
        ),
        out_shape=jax.ShapeDtypeStruct((T, D), F32),
        compiler_params=pltpu.CompilerParams(dimension_semantics=("arbitrary",), vmem_limit_bytes=VMEM_LIMIT),
        name="combine",
    )(pos1, pos2, x, wts, final_g.reshape(1, D), y3)


def _moe(x, g, w_router, w_gate, w_up, w_down, final_g, *, tm, tmg, tf):
    T, D = x.shape
    E = N_EXPERTS
    h3, keys, wts, cnt = _router(x, g, w_router, tm=tm)
    keys = keys.reshape(T // tm, SUBLANES, tm).astype(I32)
    key1, key2 = keys[:, 0, :].reshape(T), keys[:, 1, :].reshape(T)

    counts = cnt[0, :E]
    tiles = (counts + (tmg - 1)) // tmg
    tile_end = jnp.cumsum(tiles).astype(I32)
    starts = ((tile_end - tiles) * tmg).astype(I32)
    n_tiles = 2 * T // tmg + E
    n_used = tile_end[-1:]
    pad_start = (starts + counts).astype(I32)
    pad_len = (tiles * tmg - counts).astype(I32)
    def position(key):
        pos = key & KEY_MASK
        for e in range(E):
            pos = pos + jnp.where((key >> KEY_SHIFT) == e, starts[e], 0)
        return pos

    pos1, pos2 = position(key1), position(key2)

    xs3 = _dispatch(pos1, pos2, pad_start, pad_len, n_used, h3, n_tiles * tmg, tm=tm, tmg=tmg)
    y3 = _experts(tile_end, n_used, xs3, w_gate, w_up, w_down, tmg=tmg, tf=tf)
    return _combine(pos1, pos2, x, wts, final_g, y3, tm=tm)


def kernel(x, attn_norm, ffn_norm, w_in, b_in, conv_w, conv_b, conv_ln_g, conv_ln_b, sinks, w_out, b_out,
           ffn_w_gate, ffn_w_up, ffn_w_down, moe_router, moe_w_gate, moe_w_up, moe_w_down, final_norm):
    B, S, D = x.shape
    ts = min(S, 512)
    tm = min(B * S, 512)

    def mixer(x, l):
        return _mixer(x, attn_norm[l], w_in[l], b_in[l], conv_w[l], conv_b[l], conv_ln_g[l], conv_ln_b[l],
                      sinks[l], w_out[l], b_out[l], ts=ts)

    x = mixer(x, 0)
    x = _ffn(x.reshape(B * S, D), ffn_norm[0], ffn_w_gate[0], ffn_w_up[0], ffn_w_down[0],
             tm=tm)
    x = mixer(x.reshape(B, S, D), 1)
    x = _moe(x.reshape(B * S, D), ffn_norm[1], moe_router[0], moe_w_gate[0], moe_w_up[0], moe_w_down[0],
             final_norm, tm=tm, tmg=tm, tf=moe_w_gate.shape[3] // 2)
    return x.reshape(B, S, D)
```
